```python
import math, functools
import jax, jax.numpy as jnp
from jax import lax
import numpy as np

D_MODEL = 1024
BATCH = 32
SEQ = 256
DEPTH = 2
DEC_BATCH = 2
DEC_SEQ = 1024
PAST_LEN = 256

GRID_W = 64
NA_HEADS = 8
NA_HEAD_DIM = 64
NA_WIDTH = NA_HEADS * NA_HEAD_DIM
NA_WIN_ROWS_MAX = 8
NA_WIN_COLS = 16
NA_QBLOCK_COLS = NA_WIN_COLS
NA_KBAND_COLS = 2 * NA_WIN_COLS
ROPE_BASE = 10000.0
CTX_QBLOCK = 128
SSM_HEADS = 8
SSM_HEAD_DIM = 64
SSM_INNER = SSM_HEADS * SSM_HEAD_DIM
SSM_GROUPS = 2
SSM_STATE = 64
SSM_CONV = 3
SSM_CONV_DIM = SSM_INNER + 2 * SSM_GROUPS * SSM_STATE
SSM_CHUNK = 128
GMLP_GROUPS = 4
GMLP_WIDTH = 512
GMLP_CHUNK = 128
GMLP_GROUP_DIM = GMLP_WIDTH // GMLP_GROUPS
N_BRANCHES = 3
MOE_GROUPS = 4
MOE_EXPERTS_PER_GROUP = 8
MOE_EXPERTS = MOE_GROUPS * MOE_EXPERTS_PER_GROUP
MOE_TOP_K = 2
MOE_D_FF = 256
DEEPNORM_ALPHA = (2 * DEPTH) ** 0.25
DEEPNORM_BETA = (8 * DEPTH) ** -0.25
LN_EPS = 1e-5
RMS_EPS = 1e-5
OFF_Q = 0
OFF_K = OFF_Q + NA_WIDTH
OFF_V = OFF_K + NA_WIDTH
OFF_Z = OFF_V + NA_WIDTH
OFF_XBC = OFF_Z + SSM_INNER
OFF_DT = OFF_XBC + SSM_CONV_DIM
OFF_U = OFF_DT + 2 * SSM_HEADS
OFF_GV = OFF_U + GMLP_WIDTH
OFF_GATE = OFF_GV + GMLP_WIDTH
N_IN = OFF_GATE + N_BRANCHES * D_MODEL

kernel_name = "hybrid_diffusion_prefix_step"


def layer_norm(x, g, b):
    xf = x.astype(jnp.float32)
    mu = jnp.mean(xf, -1, keepdims=True)
    var = jnp.mean(jnp.square(xf - mu), -1, keepdims=True)
    y = (xf - mu) * lax.rsqrt(var + LN_EPS)
    return (y * g.astype(jnp.float32) + b.astype(jnp.float32)).astype(x.dtype)


def rms_norm(x, g):
    xf = x.astype(jnp.float32)
    y = xf * lax.rsqrt(jnp.mean(jnp.square(xf), -1, keepdims=True) + RMS_EPS)
    return (y * g.astype(jnp.float32)).astype(x.dtype)


def to_heads(t, n_heads):
    b, l, w = t.shape
    return t.reshape(b, l, n_heads, w // n_heads).transpose(0, 2, 1, 3)


def from_heads(t):
    b, h, l, d = t.shape
    return t.transpose(0, 2, 1, 3).reshape(b, l, h * d)


def axial_rope(x):
    t = x.shape[2]
    quarter = NA_HEAD_DIM // 4
    half = NA_HEAD_DIM // 2
    freqs = ROPE_BASE ** (-jnp.arange(quarter, dtype=jnp.float32) / quarter)
    pos = jnp.arange(t)
    def rotate(xa, p):
        ang = p.astype(jnp.float32)[:, None] * freqs[None, :]
        cos = jnp.cos(ang).astype(xa.dtype)
        sin = jnp.sin(ang).astype(xa.dtype)
        x1, x2 = xa[..., :quarter], xa[..., quarter:]
        return jnp.concatenate([x1 * cos - x2 * sin, x1 * sin + x2 * cos], -1)
    return jnp.concatenate([rotate(x[..., :half], pos // GRID_W),
                            rotate(x[..., half:], pos % GRID_W)], -1)


def context_attention(q, k, v):
    b, h, l, d = q.shape
    scale = d ** -0.5
    qb = jnp.moveaxis(q.reshape(b, h, l // CTX_QBLOCK, CTX_QBLOCK, d), 2, 0)
    def block(qq):
        s = jnp.einsum('bhqd,bhkd->bhqk', qq, k).astype(jnp.float32) * scale
        p = jax.nn.softmax(s, -1).astype(v.dtype)
        return jnp.einsum('bhqk,bhkd->bhqd', p, v)
    o = lax.map(block, qb)
    return from_heads(jnp.moveaxis(o, 0, 2).reshape(b, h, l, d))


def neighbourhood_attention(q, k, v, k_ctx, v_ctx, rpb):
    b, h, t, d = q.shape
    rows = t // GRID_W
    win_r = min(NA_WIN_ROWS_MAX, rows)
    n_cb = GRID_W // NA_QBLOCK_COLS
    q = axial_rope(q)
    k = axial_rope(k)
    r = np.arange(rows)
    key_rows = np.clip(r - win_r // 2, 0, rows - win_r)[:, None] + np.arange(win_r)[None, :]
    jb = np.arange(n_cb)
    key_cols = (np.clip(jb * NA_QBLOCK_COLS - NA_WIN_COLS // 2, 0, GRID_W - NA_KBAND_COLS)[:, None]
                + np.arange(NA_KBAND_COLS)[None, :])
    q_cols = jb[:, None] * NA_QBLOCK_COLS + np.arange(NA_QBLOCK_COLS)[None, :]
    col0 = np.clip(q_cols - NA_WIN_COLS // 2, 0, GRID_W - NA_WIN_COLS)
    valid = ((key_cols[:, None, :] >= col0[:, :, None])
             & (key_cols[:, None, :] < col0[:, :, None] + NA_WIN_COLS))
    n_win = win_r * NA_KBAND_COLS
    key_idx = (key_rows[:, None, :, None] * GRID_W + key_cols[None, :, None, :]).reshape(rows, n_cb, n_win)
    dr_idx = key_rows - r[:, None] + NA_WIN_ROWS_MAX - 1
    dc_idx = np.clip(key_cols[:, None, :] - q_cols[:, :, None] + NA_WIN_COLS - 1, 0, 2 * NA_WIN_COLS - 2)
    bias = rpb[:, dr_idx[:, None, None, :, None], dc_idx[None, :, :, None, :]].astype(jnp.float32)
    bias = jnp.where(valid[None, None, :, :, None, :], bias, -jnp.inf).reshape(h, rows, n_cb, NA_QBLOCK_COLS, n_win)
    qb = q.reshape(b, h, rows, n_cb, NA_QBLOCK_COLS, d)
    kg = k[:, :, key_idx]
    vg = v[:, :, key_idx]
    scale = d ** -0.5
    s_win = jnp.einsum('bhrjqd,bhrjkd->bhrjqk', qb, kg).astype(jnp.float32) * scale + bias[None]
    s_ctx = jnp.einsum('bhrjqd,bhcd->bhrjqc', qb, k_ctx).astype(jnp.float32) * scale
    p = jax.nn.softmax(jnp.concatenate([s_win, s_ctx], -1), -1).astype(v.dtype)
    o = (jnp.einsum('bhrjqk,bhrjkd->bhrjqd', p[..., :n_win], vg)
         + jnp.einsum('bhrjqc,bhcd->bhrjqd', p[..., n_win:], v_ctx))
    return from_heads(o.reshape(b, h, t, d))


def conv_centred(x, w, bias):
    pad = SSM_CONV // 2
    y = lax.conv_general_dilated(x, w[:, None, :].astype(x.dtype), (1,), [(pad, pad)],
                                 dimension_numbers=('NWC', 'WIO', 'NWC'),
                                 feature_group_count=x.shape[-1])
    return y + bias


def ssd_scan(x, dt, a, bm, cm, h0):
    b, l, h, p = x.shape
    n = bm.shape[-1]
    nc = l // SSM_CHUNK
    xc = x.reshape(b, nc, SSM_CHUNK, h, p)
    dtc = dt.reshape(b, nc, SSM_CHUNK, h)
    bc = bm.reshape(b, nc, SSM_CHUNK, h, n)
    cc = cm.reshape(b, nc, SSM_CHUNK, h, n)
    acs = jnp.cumsum(dtc * a, axis=2)
    seg = acs[:, :, :, None, :] - acs[:, :, None, :, :]
    lower = np.tril(np.ones((SSM_CHUNK, SSM_CHUNK), dtype=bool))[None, None, :, :, None]
    decay = jnp.exp(jnp.where(lower, seg, -jnp.inf))
    xdt = xc * dtc[..., None]
    scores = jnp.einsum('bcihn,bcjhn->bcijh', cc, bc) * decay
    y_diag = jnp.einsum('bcijh,bcjhp->bcihp', scores, xdt)
    decay_to_end = jnp.exp(acs[:, :, -1:, :] - acs)
    states = jnp.einsum('bcjhn,bcjh,bcjhp->bchpn', bc, decay_to_end, xdt)
    chunk_decay = jnp.exp(acs[:, :, -1, :])
    def step(h_prev, inp):
        s, dcy = inp
        return h_prev * dcy[..., None, None] + s, h_prev
    h_fin, h_in = lax.scan(step, h0.astype(states.dtype),
                           (jnp.swapaxes(states, 0, 1), jnp.swapaxes(chunk_decay, 0, 1)))
    h_in = jnp.swapaxes(h_in, 0, 1)
    y_off = jnp.einsum('bcihn,bchpn,bcih->bcihp', cc, h_in, jnp.exp(acs))
    return (y_diag + y_off).reshape(b, l, h, p), h_fin


def ssm_branch(z, xbc, dt_raw, conv_w, conv_b, a_log, dt_bias, d_skip, norm_g, h0):
    b, l, _ = xbc.shape
    xbc = jax.nn.silu(conv_centred(xbc, conv_w, conv_b))
    gn = SSM_GROUPS * SSM_STATE
    rep = SSM_HEADS // SSM_GROUPS
    xs = xbc[..., :SSM_INNER].reshape(b, l, SSM_HEADS, SSM_HEAD_DIM)
    bm = jnp.repeat(xbc[..., SSM_INNER:SSM_INNER + gn].reshape(b, l, SSM_GROUPS, SSM_STATE), rep, axis=2)
    cm = jnp.repeat(xbc[..., SSM_INNER + gn:].reshape(b, l, SSM_GROUPS, SSM_STATE), rep, axis=2)
    dt = jax.nn.softplus(dt_raw.reshape(b, l, 2, SSM_HEADS) + dt_bias)
    a = -jnp.exp(a_log)
    y_f, h_f = ssd_scan(xs, dt[:, :, 0], a[0], bm, cm, h0[:, 0])
    y_b, h_b = ssd_scan(jnp.flip(xs, 1), jnp.flip(dt[:, :, 1], 1), a[1],
                        jnp.flip(bm, 1), jnp.flip(cm, 1), h0[:, 1])
    y = (y_f + jnp.flip(y_b, 1) + xs * d_skip[:, None]).reshape(b, l, SSM_INNER)
    y = rms_norm(y * jax.nn.silu(z), norm_g)
    return y, jnp.stack([h_f, h_b], axis=1)


def gmlp_branch(u, v, ln_g, ln_b, ws, bs):
    b, l, _ = v.shape
    u = jax.nn.gelu(u, approximate=False)
    v = layer_norm(jax.nn.gelu(v, approximate=False), ln_g, ln_b)
    vc = v.reshape(b, l // GMLP_CHUNK, GMLP_CHUNK, GMLP_GROUPS, GMLP_GROUP_DIM)
    s = jnp.einsum('gij,bcjgd->bcigd', ws, vc) + bs.T[:, :, None]
    return u * s.reshape(b, l, GMLP_WIDTH)


def token_mixer(h, attend, h0, w_in, conv_w, conv_b, a_log, dt_bias, d_skip, norm_g,
                gln_g, gln_b, ws, bs, w_branch, w_o):
    b, l, _ = h.shape
    proj = h @ w_in
    q = to_heads(proj[..., OFF_Q:OFF_K], NA_HEADS)
    k = to_heads(proj[..., OFF_K:OFF_V], NA_HEADS)
    v = to_heads(proj[..., OFF_V:OFF_Z], NA_HEADS)
    y_att = attend(q, k, v)
    y_ssm, h_fin = ssm_branch(proj[..., OFF_Z:OFF_XBC], proj[..., OFF_XBC:OFF_DT], proj[..., OFF_DT:OFF_U],
                              conv_w, conv_b, a_log, dt_bias, d_skip, norm_g, h0)
    y_gm = gmlp_branch(proj[..., OFF_U:OFF_GV], proj[..., OFF_GV:OFF_GATE], gln_g, gln_b, ws, bs)
    gates = jax.nn.sigmoid(proj[..., OFF_GATE:N_IN].reshape(b, l, N_BRANCHES, D_MODEL))
    merged = (gates[:, :, 0] * (y_att @ w_branch[0])
              + gates[:, :, 1] * (y_ssm @ w_branch[1])
              + gates[:, :, 2] * (y_gm @ w_branch[2]))
    return merged @ w_o, k, v, h_fin


def hier_moe(h, w_coarse, w_fine, w1, w3, w2):
    b, l, d = h.shape
    x = h.reshape(b * l, d)
    p_coarse = jax.nn.softmax((x @ w_coarse).astype(jnp.float32), -1)
    p_g, g_sel = lax.top_k(p_coarse, 1)
    fine = (x @ w_fine).astype(jnp.float32).reshape(-1, MOE_GROUPS, MOE_EXPERTS_PER_GROUP)
    fine_g = jnp.take_along_axis(fine, g_sel[:, :, None], axis=1)[:, 0]
    top_v, top_i = lax.top_k(fine_g, MOE_TOP_K)
    w_top = jax.nn.softmax(top_v, -1) * p_g
    expert = g_sel * MOE_EXPERTS_PER_GROUP + top_i
    gates = jnp.einsum('tk,tke->te', w_top,
                       jax.nn.one_hot(expert, MOE_EXPERTS, dtype=jnp.float32)).astype(x.dtype)
    hid = jax.nn.silu(jnp.einsum('td,edf->tef', x, w1)) * jnp.einsum('td,edf->tef', x, w3)
    y = jnp.einsum('tef,efd->td', hid * gates[:, :, None], w2)
    return y.reshape(b, l, d)


def _normal(k, shape, scale):
    return jax.random.normal(k, shape, jnp.float32) * scale


def setup_inputs(seed: int = 0) -> dict:
    key = jax.random.key(seed)
    ks = jax.random.split(key, 40)
    dt0 = jnp.exp(jax.random.uniform(ks[13], (DEPTH, 2, SSM_HEADS), jnp.float32,
                                     minval=math.log(1e-3), maxval=math.log(1e-1)))
    return {
        "x_prompt": _normal(ks[0], (BATCH, SEQ, D_MODEL), 1.0),
        "x_sample": _normal(ks[1], (DEC_BATCH, DEC_SEQ, D_MODEL), 1.0),
        "cache_k": _normal(ks[2], (DEC_BATCH, DEPTH, NA_HEADS, PAST_LEN, NA_HEAD_DIM), 1.0),
        "cache_v": _normal(ks[3], (DEC_BATCH, DEPTH, NA_HEADS, PAST_LEN, NA_HEAD_DIM), 1.0),
        "state_ssm": _normal(ks[4], (DEC_BATCH, DEPTH, 2, SSM_HEADS, SSM_HEAD_DIM, SSM_STATE), 0.5),
        "c": _normal(ks[5], (DEC_BATCH, D_MODEL), 1.0),
        "c_ctx": _normal(ks[6], (D_MODEL,), 1.0),
        "w_ada": _normal(ks[7], (DEPTH, D_MODEL, 6 * D_MODEL), D_MODEL ** -0.5),
        "b_ada": _normal(ks[8], (DEPTH, 6 * D_MODEL), 0.02),
        "w_in": _normal(ks[9], (DEPTH, D_MODEL, N_IN), D_MODEL ** -0.5),
        "na_rpb": _normal(ks[10], (DEPTH, NA_HEADS, 2 * NA_WIN_ROWS_MAX - 1, 2 * NA_WIN_COLS - 1), 0.1),
        "ssm_conv_w": _normal(ks[11], (DEPTH, SSM_CONV, SSM_CONV_DIM), SSM_CONV ** -0.5),
        "ssm_conv_b": _normal(ks[12], (DEPTH, SSM_CONV_DIM), 0.02),
        "ssm_a_log": jnp.log(jax.random.uniform(ks[14], (DEPTH, 2, SSM_HEADS), jnp.float32, minval=1.0, maxval=16.0)),
        "ssm_dt_bias": dt0 + jnp.log(-jnp.expm1(-dt0)),
        "ssm_d": 1.0 + _normal(ks[15], (DEPTH, SSM_HEADS), 0.1),
        "ssm_norm_g": 1.0 + _normal(ks[16], (DEPTH, SSM_INNER), 0.1),
        "gmlp_ln_g": 1.0 + _normal(ks[17], (DEPTH, GMLP_WIDTH), 0.1),
        "gmlp_ln_b": _normal(ks[18], (DEPTH, GMLP_WIDTH), 0.02),
        "gmlp_ws": _normal(ks[19], (DEPTH, GMLP_GROUPS, GMLP_CHUNK, GMLP_CHUNK), GMLP_CHUNK ** -0.5),
        "gmlp_bs": 1.0 + _normal(ks[20], (DEPTH, GMLP_GROUPS, GMLP_CHUNK), 0.1),
        "w_branch": _normal(ks[21], (DEPTH, N_BRANCHES, NA_WIDTH, D_MODEL), NA_WIDTH ** -0.5),
        "w_o": _normal(ks[22], (DEPTH, D_MODEL, D_MODEL), D_MODEL ** -0.5 * DEEPNORM_BETA),
        "ln_g": 1.0 + _normal(ks[23], (DEPTH, 2, D_MODEL), 0.1),
        "ln_b": _normal(ks[24], (DEPTH, 2, D_MODEL), 0.02),
        "moe_w_coarse": _normal(ks[25], (DEPTH, D_MODEL, MOE_GROUPS), D_MODEL ** -0.5),
        "moe_w_fine": _normal(ks[26], (DEPTH, D_MODEL, MOE_EXPERTS), D_MODEL ** -0.5),
        "moe_w1": _normal(ks[27], (DEPTH, MOE_EXPERTS, D_MODEL, MOE_D_FF), D_MODEL ** -0.5),
        "moe_w3": _normal(ks[28], (DEPTH, MOE_EXPERTS, D_MODEL, MOE_D_FF), D_MODEL ** -0.5),
        "moe_w2": _normal(ks[29], (DEPTH, MOE_EXPERTS, MOE_D_FF, D_MODEL), MOE_D_FF ** -0.5 * DEEPNORM_BETA),
    }


def reference(x_prompt, x_sample, cache_k, cache_v, state_ssm, c, c_ctx, w_ada, b_ada, w_in, na_rpb,
              ssm_conv_w, ssm_conv_b, ssm_a_log, ssm_dt_bias, ssm_d, ssm_norm_g, gmlp_ln_g, gmlp_ln_b,
              gmlp_ws, gmlp_bs, w_branch, w_o, ln_g, ln_b, moe_w_coarse, moe_w_fine, moe_w1, moe_w3, moe_w2):
    xp = x_prompt
    bsz = xp.shape[0]
    ks_out, vs_out, ss_out = [], [], []
    for l in range(DEPTH):
        mix_p = (w_in[l], ssm_conv_w[l], ssm_conv_b[l], ssm_a_log[l], ssm_dt_bias[l], ssm_d[l], ssm_norm_g[l],
                 gmlp_ln_g[l], gmlp_ln_b[l], gmlp_ws[l], gmlp_bs[l], w_branch[l], w_o[l])
        mod = jax.nn.silu(c_ctx) @ w_ada[l] + b_ada[l]
        sh1, sc1, g1, sh2, sc2, g2 = jnp.split(mod, 6, axis=-1)
        h0 = jnp.zeros((bsz, 2, SSM_HEADS, SSM_HEAD_DIM, SSM_STATE), xp.dtype)
        out, k, v, h_fin = token_mixer(xp * (1 + sc1) + sh1, context_attention, h0, *mix_p)
        xp = layer_norm(DEEPNORM_ALPHA * xp + g1 * out, ln_g[l, 0], ln_b[l, 0])
        ff = hier_moe(xp * (1 + sc2) + sh2, moe_w_coarse[l], moe_w_fine[l], moe_w1[l], moe_w3[l], moe_w2[l])
        xp = layer_norm(DEEPNORM_ALPHA * xp + g2 * ff, ln_g[l, 1], ln_b[l, 1])
        ks_out.append(k)
        vs_out.append(v)
        ss_out.append(h_fin)
    xs = x_sample
    for l in range(DEPTH):
        mix_p = (w_in[l], ssm_conv_w[l], ssm_conv_b[l], ssm_a_log[l], ssm_dt_bias[l], ssm_d[l], ssm_norm_g[l],
                 gmlp_ln_g[l], gmlp_ln_b[l], gmlp_ws[l], gmlp_bs[l], w_branch[l], w_o[l])
        mod = (jax.nn.silu(c) @ w_ada[l] + b_ada[l])[:, None, :]
        sh1, sc1, g1, sh2, sc2, g2 = jnp.split(mod, 6, axis=-1)
        attend = functools.partial(neighbourhood_attention, k_ctx=cache_k[:, l], v_ctx=cache_v[:, l], rpb=na_rpb[l])
        out, _, _, _ = token_mixer(xs * (1 + sc1) + sh1, attend, state_ssm[:, l], *mix_p)
        xs = layer_norm(DEEPNORM_ALPHA * xs + g1 * out, ln_g[l, 0], ln_b[l, 0])
        ff = hier_moe(xs * (1 + sc2) + sh2, moe_w_coarse[l], moe_w_fine[l], moe_w1[l], moe_w3[l], moe_w2[l])
        xs = layer_norm(DEEPNORM_ALPHA * xs + g2 * ff, ln_g[l, 1], ln_b[l, 1])
    new_cache_k = jnp.stack(ks_out, axis=1)
    new_cache_v = jnp.stack(vs_out, axis=1)
    new_state_ssm = jnp.stack(ss_out, axis=1)
    return (xp, xs, new_cache_k, new_cache_v, new_state_ssm)
```

```python
import functools
import math

import numpy as np
import jax
import jax.numpy as jnp
from jax import lax
from jax.experimental import pallas as pl
from jax.experimental.pallas import tpu as pltpu

f32 = jnp.float32
bf16 = jnp.bfloat16
i32 = jnp.int32

D_MODEL = 1024
BATCH = 32
SEQ = 256
DEPTH = 2
DEC_BATCH = 2
DEC_SEQ = 1024
PAST_LEN = 256
GRID_W = 64
NA_HEADS = 8
NA_HEAD_DIM = 64
NA_WIDTH = NA_HEADS * NA_HEAD_DIM
NA_WIN_ROWS = 8
NA_WIN_COLS = 16
ROPE_BASE = 10000.0
SSM_HEADS = 8
SSM_HEAD_DIM = 64
SSM_INNER = SSM_HEADS * SSM_HEAD_DIM
SSM_GROUPS = 2
SSM_STATE = 64
SSM_CONV_DIM = SSM_INNER + 2 * SSM_GROUPS * SSM_STATE
CHUNK = 128
GMLP_GROUPS = 4
GMLP_WIDTH = 512
N_BRANCHES = 3
MOE_GROUPS = 4
MOE_EPG = 8
MOE_EXPERTS = MOE_GROUPS * MOE_EPG
MOE_D_FF = 256
DEEPNORM_ALPHA = (2 * DEPTH) ** 0.25
LN_EPS = 1e-5
RMS_EPS = 1e-5

OFF_Q = 0
OFF_Z = 3 * NA_WIDTH
OFF_XBC = OFF_Z + SSM_INNER
OFF_DT = OFF_XBC + SSM_CONV_DIM
OFF_U = OFF_DT + 2 * SSM_HEADS
OFF_GATE = OFF_U + 2 * GMLP_WIDTH
N_IN = OFF_GATE + N_BRANCHES * D_MODEL

LANES = 128
P_Q, P_K, P_V, P_Z, P_X, P_U, P_GV = 0, 512, 1024, 1536, 2048, 2560, 3072
P_B, P_C, P_DT = 3584, 3712, 3840
N_PROJ = 3968

T_CTX = BATCH * SEQ
T_DN = DEC_BATCH * DEC_SEQ
T_ALL = T_CTX + T_DN
N_ROWTYPES = 8

TM = 256
TME = 128
N_PAIRS = 2 * T_ALL
N_ETILES = (N_PAIRS + MOE_EXPERTS * (TME - 1) + TME - 1) // TME
P_PAD = N_ETILES * TME

VMEM_LIMIT = 56 * 1024 * 1024


def _cparams(n_axes):
    return pltpu.CompilerParams(dimension_semantics=("arbitrary",) * n_axes, vmem_limit_bytes=VMEM_LIMIT)


def _row_type(tile, rows_per_tile):
    start = tile * rows_per_tile
    return jnp.where(start < T_CTX, 0, 1 + (start - T_CTX) // DEC_SEQ)


def _silu(x):
    return x * jax.nn.sigmoid(x)


def _gelu(x):
    return 0.5 * x * (1.0 + lax.erf(x * (1.0 / math.sqrt(2.0))))


def _dot(a, b):
    return jnp.dot(a, b, preferred_element_type=f32)


def _dot_nt(a, b):
    return lax.dot_general(a, b, (((1,), (1,)), ((), ())), preferred_element_type=f32)


def _dot_tn(a, b):
    return lax.dot_general(a, b, (((0,), (0,)), ((), ())), preferred_element_type=f32)


def _mods_kernel(c_ref, w_ref, b_ref, o_ref):
    o_ref[...] = _dot(_silu(c_ref[...]).astype(bf16), w_ref[...].astype(bf16)) + b_ref[...]


def _mods(cvec, w_ada, b_ada):
    tn = 1536
    return pl.pallas_call(
        _mods_kernel,
        grid=(DEPTH, 6 * D_MODEL // tn),
        in_specs=[
            pl.BlockSpec((N_ROWTYPES, D_MODEL), lambda l, j: (0, 0)),
            pl.BlockSpec((None, D_MODEL, tn), lambda l, j: (l, 0, j)),
            pl.BlockSpec((None, 1, tn), lambda l, j: (l, 0, j)),
        ],
        out_specs=pl.BlockSpec((None, N_ROWTYPES, tn), lambda l, j: (l, 0, j)),
        out_shape=jax.ShapeDtypeStruct((DEPTH, N_ROWTYPES, 6 * D_MODEL), f32),
        compiler_params=_cparams(2),
        name="mods",
    )(cvec, w_ada, b_ada.reshape(DEPTH, 1, 6 * D_MODEL))


def _inproj_kernel(x_ref, mod_ref, w_ref, o_ref):
    m = mod_ref[pl.ds(_row_type(pl.program_id(0), TM), 1), :]
    h = x_ref[...] * (1.0 + m[:, D_MODEL:2 * D_MODEL]) + m[:, 0:D_MODEL]
    o_ref[...] = _dot(h.astype(bf16), w_ref[...])


def _inproj(x, mod, w_proj):
    return pl.pallas_call(
        _inproj_kernel,
        grid=(T_ALL // TM,),
        in_specs=[
            pl.BlockSpec((TM, D_MODEL), lambda i: (i, 0)),
            pl.BlockSpec((N_ROWTYPES, 6 * D_MODEL), lambda i: (0, 0)),
            pl.BlockSpec((D_MODEL, N_PROJ), lambda i: (0, 0)),
        ],
        out_specs=pl.BlockSpec((TM, N_PROJ), lambda i: (i, 0)),
        out_shape=jax.ShapeDtypeStruct((T_ALL, N_PROJ), f32),
        compiler_params=_cparams(1),
        name="inproj",
    )(x, mod, w_proj)


def _ctx_attn_kernel(q_ref, k_ref, v_ref, y_ref, kc_ref, vc_ref):
    scale = NA_HEAD_DIM ** -0.5
    outs = []
    for h in range(NA_HEADS):
        sl = slice(h * NA_HEAD_DIM, (h + 1) * NA_HEAD_DIM)
        q, k, v = q_ref[:, sl], k_ref[:, sl], v_ref[:, sl]
        kc_ref[h] = k
        vc_ref[h] = v
        s = _dot_nt(q.astype(bf16), k.astype(bf16)) * scale
        e = jnp.exp(s - jnp.max(s, -1, keepdims=True))
        p = e / jnp.sum(e, -1, keepdims=True)
        outs.append(_dot(p.astype(bf16), v.astype(bf16)))
    y_ref[...] = jnp.concatenate(outs, axis=1)


def _ctx_attn(proj, layer, kc_prev, vc_prev):
    cache_shape = jax.ShapeDtypeStruct((BATCH, DEPTH, NA_HEADS, SEQ, NA_HEAD_DIM), f32)
    in_specs = [
        pl.BlockSpec((SEQ, NA_WIDTH), lambda b: (b, P_Q // NA_WIDTH)),
        pl.BlockSpec((SEQ, NA_WIDTH), lambda b: (b, P_K // NA_WIDTH)),
        pl.BlockSpec((SEQ, NA_WIDTH), lambda b: (b, P_V // NA_WIDTH)),
    ]
    args = [proj, proj, proj]
    aliases = {}
    kernel = _ctx_attn_kernel
    if kc_prev is not None:
        in_specs += [pl.BlockSpec(memory_space=pl.ANY)] * 2
        args += [kc_prev, vc_prev]
        aliases = {3: 1, 4: 2}
        kernel = lambda q, k, v, _kc, _vc, y, kc, vc: _ctx_attn_kernel(q, k, v, y, kc, vc)
    cache_spec = pl.BlockSpec((None, None, NA_HEADS, SEQ, NA_HEAD_DIM), lambda b: (b, layer, 0, 0, 0))
    return pl.pallas_call(
        kernel,
        grid=(BATCH,),
        in_specs=in_specs,
        out_specs=[pl.BlockSpec((SEQ, NA_WIDTH), lambda b: (b, 0)), cache_spec, cache_spec],
        out_shape=[jax.ShapeDtypeStruct((T_ALL, NA_WIDTH), f32), cache_shape, cache_shape],
        input_output_aliases=aliases,
        compiler_params=_cparams(1),
        name="ctx_attn",
    )(*args)


GRID_ROWS = DEC_SEQ // GRID_W
N_WIN = NA_WIN_ROWS * GRID_W


def _rope_tables():
    quarter = NA_HEAD_DIM // 4
    freqs = ROPE_BASE ** (-np.arange(quarter, dtype=np.float32) / quarter)
    lane = np.arange(LANES)
    j = lane % NA_HEAD_DIM
    use_col = (j // (NA_HEAD_DIM // 2)) == 1
    fi = j % quarter
    second = (j % (NA_HEAD_DIM // 2)) >= quarter
    pos = jnp.arange(DEC_SEQ)
    p = jnp.where(use_col[None, :], (pos % GRID_W)[:, None], (pos // GRID_W)[:, None]).astype(f32)
    ang = p * jnp.asarray(freqs[fi], f32)[None, :]
    cos = jnp.cos(ang)
    sin = jnp.sin(ang)
    return cos, jnp.where(second[None, :], sin, -sin)


def _na_bias(rpb):
    qc = np.arange(GRID_W)[:, None]
    kc = np.arange(GRID_W)[None, :]
    col0 = np.clip(qc - NA_WIN_COLS // 2, 0, GRID_W - NA_WIN_COLS)
    valid = (kc >= col0) & (kc < col0 + NA_WIN_COLS)
    dc = np.clip(kc - qc + NA_WIN_COLS - 1, 0, 2 * NA_WIN_COLS - 2)
    tb = jnp.where(valid[None, None], rpb[:, :, dc].astype(f32), -jnp.inf)
    rows = [jnp.concatenate([tb[:, d0 + i] for i in range(NA_WIN_ROWS)], axis=-1) for d0 in range(NA_WIN_ROWS)]
    return jnp.stack(rows, axis=1)


def _rope(x, cos, sin_signed):
    lane = lax.broadcasted_iota(i32, x.shape, 1)
    first = (lane % (NA_HEAD_DIM // 2)) < (NA_HEAD_DIM // 4)
    q = NA_HEAD_DIM // 4
    partner = jnp.where(first, pltpu.roll(x, LANES - q, 1), pltpu.roll(x, q, 1))
    return x * cos + partner * sin_signed


def _na_attn_kernel(q_ref, k_ref, v_ref, kctx_ref, vctx_ref, bias_ref, cos_ref, sin_ref, _y_in, y_ref,
                    q_scr, k_scr, v_scr, sc_scr, y_scr):
    scale = NA_HEAD_DIM ** -0.5
    cos, sin = cos_ref[...], sin_ref[...]
    q = _rope(q_ref[...], cos, sin).astype(bf16)
    k = _rope(k_ref[...], cos, sin).astype(bf16)
    v = v_ref[...].astype(bf16)
    for hh in range(2):
        sl = slice(hh * NA_HEAD_DIM, (hh + 1) * NA_HEAD_DIM)
        q_scr[hh] = q[:, sl]
        k_scr[hh] = k[:, sl]
        v_scr[hh] = v[:, sl]
        sc_scr[hh] = _dot_nt(q[:, sl], kctx_ref[hh].astype(bf16)) * scale
    for hh in range(2):
        vctx = vctx_ref[hh].astype(bf16)

        def row_body(r, carry):
            kr0 = jnp.clip(r - NA_WIN_ROWS // 2, 0, GRID_ROWS - NA_WIN_ROWS)
            d0 = kr0 - r + NA_WIN_ROWS - 1
            q0 = pl.multiple_of(r * GRID_W, GRID_W)
            k0 = pl.multiple_of(kr0 * GRID_W, GRID_W)
            qr = q_scr[hh, pl.ds(q0, GRID_W), :]
            kw = k_scr[hh, pl.ds(k0, N_WIN), :]
            vw = v_scr[hh, pl.ds(k0, N_WIN), :]
            s_win = _dot_nt(qr, kw) * scale + bias_ref[hh, d0]
            s_ctx = sc_scr[hh, pl.ds(q0, GRID_W), :]
            m = jnp.maximum(jnp.max(s_win, -1, keepdims=True), jnp.max(s_ctx, -1, keepdims=True))
            e_win = jnp.exp(s_win - m)
            e_ctx = jnp.exp(s_ctx - m)
            denom = jnp.sum(e_win, -1, keepdims=True) + jnp.sum(e_ctx, -1, keepdims=True)
            o = _dot(e_win.astype(bf16), vw) + _dot(e_ctx.astype(bf16), vctx)
            y_scr[hh, pl.ds(q0, GRID_W), :] = o / denom
            return carry

        lax.fori_loop(0, GRID_ROWS, row_body, 0)
    y_ref[...] = jnp.concatenate([y_scr[0], y_scr[1]], axis=1)


def _na_attn(proj, cache_k, cache_v, bias, cos, sin, layer, y_att):
    dn0 = T_CTX // DEC_SEQ
    blk = lambda off: pl.BlockSpec((DEC_SEQ, LANES), lambda b, hp: (dn0 + b, off // LANES + hp))
    ctx_spec = pl.BlockSpec((None, None, 2, PAST_LEN, NA_HEAD_DIM), lambda b, hp: (b, layer, hp, 0, 0))
    return pl.pallas_call(
        _na_attn_kernel,
        grid=(DEC_BATCH, NA_HEADS // 2),
        in_specs=[
            blk(P_Q), blk(P_K), blk(P_V), ctx_spec, ctx_spec,
            pl.BlockSpec((2, NA_WIN_ROWS, GRID_W, N_WIN), lambda b, hp: (hp, 0, 0, 0)),
            pl.BlockSpec((DEC_SEQ, LANES), lambda b, hp: (0, 0)),
            pl.BlockSpec((DEC_SEQ, LANES), lambda b, hp: (0, 0)),
            pl.BlockSpec(memory_space=pl.ANY),
        ],
        out_specs=pl.BlockSpec((DEC_SEQ, LANES), lambda b, hp: (dn0 + b, hp)),
        out_shape=jax.ShapeDtypeStruct((T_ALL, NA_WIDTH), f32),
        scratch_shapes=[
            pltpu.VMEM((2, DEC_SEQ, NA_HEAD_DIM), bf16),
            pltpu.VMEM((2, DEC_SEQ, NA_HEAD_DIM), bf16),
            pltpu.VMEM((2, DEC_SEQ, NA_HEAD_DIM), bf16),
            pltpu.VMEM((2, DEC_SEQ, PAST_LEN), f32),
            pltpu.VMEM((2, DEC_SEQ, NA_HEAD_DIM), f32),
        ],
        input_output_aliases={8: 0},
        compiler_params=_cparams(2),
        name="na_attn",
    )(proj, proj, proj, cache_k, cache_v, bias, cos, sin, y_att)


def _conv_silu(v, w, b):
    n = v.shape[0]
    rows = lax.broadcasted_iota(i32, v.shape, 0)
    prev = jnp.where(rows == 0, 0.0, pltpu.roll(v, 1, 0))
    nxt = jnp.where(rows == n - 1, 0.0, pltpu.roll(v, n - 1, 0))
    return _silu(prev * w[0:1] + v * w[1:2] + nxt * w[2:3] + b)


def _ssd_kernel(*refs, seq_len, has_h0, emit_state, aliased):
    refs = list(refs)
    z_ref, x_ref, b_ref, c_ref, dt_ref, cw_ref, cb_ref, alog_ref, dtb_ref, dskip_ref, ng_ref = refs[:11]
    pos = 11
    h0_ref = None
    if has_h0:
        h0_ref = refs[pos]
        pos += 1
    pos += aliased
    y_ref = refs[pos]
    pos += 1
    st_ref = None
    if emit_state:
        st_ref = refs[pos]
        pos += 1
    xs_scr, b_scr, c_scr, dt_scr, dta_scr, y_scr, h_scr = refs[pos:]

    n_chunks = seq_len // CHUNK
    cw = cw_ref[...]
    cb = cb_ref[...]
    xs = _conv_silu(x_ref[...], cw[:, 0:SSM_INNER], cb[:, 0:SSM_INNER])
    xs_scr[...] = xs
    gn = SSM_GROUPS * SSM_STATE
    b_scr[...] = _conv_silu(b_ref[...], cw[:, SSM_INNER:SSM_INNER + gn], cb[:, SSM_INNER:SSM_INNER + gn])
    c_scr[...] = _conv_silu(c_ref[...], cw[:, SSM_INNER + gn:], cb[:, SSM_INNER + gn:])
    raw = dt_ref[...] + dtb_ref[...]
    dt = jnp.maximum(raw, 0.0) + jnp.log1p(jnp.exp(-jnp.abs(raw)))
    dt_scr[...] = dt
    dta_scr[...] = dt * (-jnp.exp(alog_ref[...]))
    y_scr[...] = xs * dskip_ref[...]
    if has_h0:
        h_scr[...] = h0_ref[...]
    else:
        h_scr[...] = jnp.zeros(h_scr.shape, f32)

    ri = lax.broadcasted_iota(i32, (CHUNK, CHUNK), 0)
    ci = lax.broadcasted_iota(i32, (CHUNK, CHUNK), 1)
    keep = (ri >= ci, ri <= ci)
    rep = SSM_HEADS // SSM_GROUPS

    def one_chunk(direction, chunk):
        r0 = pl.multiple_of(chunk * CHUNK, CHUNK)
        rows = pl.ds(r0, CHUNK)
        mask = keep[direction]
        cum = jnp.dot(mask.astype(f32), dta_scr[rows, :], precision=lax.Precision.HIGHEST,
                      preferred_element_type=f32)
        cum_t = cum.T
        end = cum[CHUNK - 1:CHUNK, :] if direction == 0 else cum[0:1, :]
        grow = jnp.exp(cum)
        to_end = jnp.exp(end - cum)
        chunk_decay = jnp.exp(end)
        bc = b_scr[rows, :].astype(bf16)
        cc = c_scr[rows, :].astype(bf16)
        xc = xs_scr[rows, :]
        dtc = dt_scr[rows, :]
        ys = []
        for g in range(SSM_GROUPS):
            bg = bc[:, g * SSM_STATE:(g + 1) * SSM_STATE]
            cg = cc[:, g * SSM_STATE:(g + 1) * SSM_STATE]
            cb_g = _dot_nt(cg, bg)
            for hh in range(rep):
                h = g * rep + hh
                ln = direction * SSM_HEADS + h
                col = cum[:, ln:ln + 1]
                seg = col - cum_t[ln:ln + 1, :]
                decay = jnp.exp(jnp.where(mask, seg, -jnp.inf))
                xdt = xc[:, h * SSM_HEAD_DIM:(h + 1) * SSM_HEAD_DIM] * dtc[:, ln:ln + 1]
                h_prev = h_scr[direction, h]
                y = _dot((cb_g * decay).astype(bf16), xdt.astype(bf16))
                y = y + _dot_nt(cg, h_prev.astype(bf16)) * grow[:, ln:ln + 1]
                states = _dot_tn((xdt * to_end[:, ln:ln + 1]).astype(bf16), bg)
                h_scr[direction, h] = h_prev * chunk_decay[:, ln:ln + 1] + states
                ys.append(y)
        y_scr[rows, :] = y_scr[rows, :] + jnp.concatenate(ys, axis=1)

    def step(s, carry):
        one_chunk(0, s)
        one_chunk(1, n_chunks - 1 - s)
        return carry

    lax.fori_loop(0, n_chunks, step, 0)

    z = z_ref[...]
    y = y_scr[...] * _silu(z)
    y = y * lax.rsqrt(jnp.mean(jnp.square(y), -1, keepdims=True) + RMS_EPS)
    y_ref[...] = y * ng_ref[...]
    if emit_state:
        st_ref[...] = h_scr[...]


def _ssd(proj, params, layer, *, latent, h0=None, y_prev=None, st_prev=None):
    seq_len = DEC_SEQ if latent else SEQ
    n_seq = DEC_BATCH if latent else BATCH
    blk0 = (T_CTX // DEC_SEQ) if latent else 0
    blk = lambda width, off: pl.BlockSpec((seq_len, width), lambda b: (blk0 + b, off // width))
    const = lambda shape: pl.BlockSpec(shape, lambda b: (0,) * len(shape))
    conv_w, conv_b, a_log_row, dt_bias_row, d_skip_row, norm_g = params
    in_specs = [
        blk(SSM_INNER, P_Z), blk(SSM_INNER, P_X), blk(LANES, P_B), blk(LANES, P_C), blk(LANES, P_DT),
        const(conv_w.shape), const(conv_b.shape), const(a_log_row.shape), const(dt_bias_row.shape),
        const(d_skip_row.shape), const(norm_g.shape),
    ]
    args = [proj, proj, proj, proj, proj, conv_w, conv_b, a_log_row, dt_bias_row, d_skip_row, norm_g]
    state_block = (None, None, 2, SSM_HEADS, SSM_HEAD_DIM, SSM_STATE)
    if h0 is not None:
        in_specs.append(pl.BlockSpec(state_block, lambda b: (b, layer, 0, 0, 0, 0)))
        args.append(h0)
    aliases = {}
    n_alias = 0
    out_specs = [pl.BlockSpec((seq_len, SSM_INNER), lambda b: (blk0 + b, 0))]
    out_shape = [jax.ShapeDtypeStruct((T_ALL, SSM_INNER), f32)]
    if y_prev is not None:
        aliases[len(args)] = 0
        in_specs.append(pl.BlockSpec(memory_space=pl.ANY))
        args.append(y_prev)
        n_alias += 1
    emit_state = not latent
    if emit_state:
        out_specs.append(pl.BlockSpec(state_block, lambda b: (b, layer, 0, 0, 0, 0)))
        out_shape.append(jax.ShapeDtypeStruct((BATCH, DEPTH, 2, SSM_HEADS, SSM_HEAD_DIM, SSM_STATE), f32))
        if st_prev is not None:
            aliases[len(args)] = 1
            in_specs.append(pl.BlockSpec(memory_space=pl.ANY))
            args.append(st_prev)
            n_alias += 1
    return pl.pallas_call(
        functools.partial(_ssd_kernel, seq_len=seq_len, has_h0=h0 is not None, emit_state=emit_state,
                          aliased=n_alias),
        grid=(n_seq,),
        in_specs=in_specs,
        out_specs=out_specs,
        out_shape=out_shape,
        scratch_shapes=[
            pltpu.VMEM((seq_len, SSM_INNER), f32),
            pltpu.VMEM((seq_len, LANES), f32),
            pltpu.VMEM((seq_len, LANES), f32),
            pltpu.VMEM((seq_len, LANES), f32),
            pltpu.VMEM((seq_len, LANES), f32),
            pltpu.VMEM((seq_len, SSM_INNER), f32),
            pltpu.VMEM((2, SSM_HEADS, SSM_HEAD_DIM, SSM_STATE), f32),
        ],
        input_output_aliases=aliases,
        compiler_params=_cparams(1),
        name="ssd_latent" if latent else "ssd_ctx",
    )(*args)


def _gmlp_kernel(u_ref, v_ref, g_ref, b_ref, ws_ref, bs_ref, o_ref):
    u = _gelu(u_ref[...])
    v = _gelu(v_ref[...])
    mu = jnp.mean(v, -1, keepdims=True)
    var = jnp.mean(jnp.square(v - mu), -1, keepdims=True)
    v = ((v - mu) * lax.rsqrt(var + LN_EPS) * g_ref[...] + b_ref[...]).astype(bf16)
    gd = GMLP_WIDTH // GMLP_GROUPS
    rows = []
    for c in range(TM // CHUNK):
        cols = [_dot(ws_ref[g].astype(bf16), v[c * CHUNK:(c + 1) * CHUNK, g * gd:(g + 1) * gd])
                for g in range(GMLP_GROUPS)]
        rows.append(jnp.concatenate(cols, axis=1) + bs_ref[...])
    o_ref[...] = u * jnp.concatenate(rows, axis=0)


def _gmlp(proj, ln_g, ln_b, ws, bs_full):
    const = lambda shape: pl.BlockSpec(shape, lambda i: (0,) * len(shape))
    return pl.pallas_call(
        _gmlp_kernel,
        grid=(T_ALL // TM,),
        in_specs=[
            pl.BlockSpec((TM, GMLP_WIDTH), lambda i: (i, P_U // GMLP_WIDTH)),
            pl.BlockSpec((TM, GMLP_WIDTH), lambda i: (i, P_GV // GMLP_WIDTH)),
            const(ln_g.shape), const(ln_b.shape), const(ws.shape), const(bs_full.shape),
        ],
        out_specs=pl.BlockSpec((TM, GMLP_WIDTH), lambda i: (i, 0)),
        out_shape=jax.ShapeDtypeStruct((T_ALL, GMLP_WIDTH), f32),
        compiler_params=_cparams(1),
        name="gmlp",
    )(proj, proj, ln_g, ln_b, ws, bs_full)


def _layer_norm(x, g, b):
    mu = jnp.mean(x, -1, keepdims=True)
    var = jnp.mean(jnp.square(x - mu), -1, keepdims=True)
    return (x - mu) * lax.rsqrt(var + LN_EPS) * g + b


def _route(logits):
    lane = lax.broadcasted_iota(i32, logits.shape, 1).astype(f32)
    big = 1e9
    lc = jnp.where(lane < MOE_GROUPS, logits, -jnp.inf)
    mc = jnp.max(lc, -1, keepdims=True)
    p_g = 1.0 / jnp.sum(jnp.exp(lc - mc), -1, keepdims=True)
    g_sel = jnp.min(jnp.where(lc == mc, lane, big), -1, keepdims=True)
    lo = MOE_GROUPS + g_sel * MOE_EPG
    fm = jnp.where((lane >= lo) & (lane < lo + MOE_EPG), logits, -jnp.inf)
    v1 = jnp.max(fm, -1, keepdims=True)
    i1 = jnp.min(jnp.where(fm == v1, lane, big), -1, keepdims=True)
    fm2 = jnp.where(lane == i1, -jnp.inf, fm)
    v2 = jnp.max(fm2, -1, keepdims=True)
    i2 = jnp.min(jnp.where(fm2 == v2, lane, big), -1, keepdims=True)
    e2 = jnp.exp(v2 - v1)
    w1 = p_g / (1.0 + e2)
    w2 = p_g * e2 / (1.0 + e2)
    out = jnp.where(lane == 0, i1 - MOE_GROUPS, 0.0)
    out = jnp.where(lane == 1, i2 - MOE_GROUPS, out)
    out = jnp.where(lane == 2, w1, out)
    return jnp.where(lane == 3, w2, out)


def _merge_kernel(x_ref, ya_ref, ys_ref, yg_ref, mod_ref, wg_ref, wb_ref, wo_ref, lng_ref, lnb_ref, wr_ref,
                  x1_ref, h2_ref, route_ref):
    m = mod_ref[pl.ds(_row_type(pl.program_id(0), TM), 1), :]
    sh1, sc1, g1 = m[:, 0:D_MODEL], m[:, D_MODEL:2 * D_MODEL], m[:, 2 * D_MODEL:3 * D_MODEL]
    sh2, sc2 = m[:, 3 * D_MODEL:4 * D_MODEL], m[:, 4 * D_MODEL:5 * D_MODEL]
    x = x_ref[...]
    h = (x * (1.0 + sc1) + sh1).astype(bf16)
    merged = None
    for i, y_ref in enumerate((ya_ref, ys_ref, yg_ref)):
        gate = jax.nn.sigmoid(_dot(h, wg_ref[:, i * D_MODEL:(i + 1) * D_MODEL]))
        term = gate * _dot(y_ref[...].astype(bf16), wb_ref[i])
        merged = term if merged is None else merged + term
    out = _dot(merged.astype(bf16), wo_ref[...])
    x1 = _layer_norm(DEEPNORM_ALPHA * x + g1 * out, lng_ref[...], lnb_ref[...])
    x1_ref[...] = x1
    h2 = x1 * (1.0 + sc2) + sh2
    h2_ref[...] = h2
    logits = jnp.dot(h2, wr_ref[...], precision=lax.Precision.HIGHEST, preferred_element_type=f32)
    route_ref[...] = _route(logits)


def _merge(x, ya, ys, yg, mod, w_gate, w_br, w_o, ln_g, ln_b, w_r):
    const = lambda shape: pl.BlockSpec(shape, lambda i: (0,) * len(shape))
    row = lambda width: pl.BlockSpec((TM, width), lambda i: (i, 0))
    return pl.pallas_call(
        _merge_kernel,
        grid=(T_ALL // TM,),
        in_specs=[
            row(D_MODEL), row(NA_WIDTH), row(SSM_INNER), row(GMLP_WIDTH), const(mod.shape),
            const(w_gate.shape), const(w_br.shape), const(w_o.shape), const(ln_g.shape), const(ln_b.shape),
            const(w_r.shape),
        ],
        out_specs=[row(D_MODEL), row(D_MODEL), row(LANES)],
        out_shape=[
            jax.ShapeDtypeStruct((T_ALL, D_MODEL), f32),
            jax.ShapeDtypeStruct((T_ALL, D_MODEL), f32),
            jax.ShapeDtypeStruct((T_ALL, LANES), f32),
        ],
        compiler_params=_cparams(1),
        name="merge",
    )(x, ya, ys, yg, mod, w_gate, w_br, w_o, ln_g, ln_b, w_r)


def _moe_kernel(tok_ref, te_ref, na_ref, h2_hbm, w1_ref, w3_ref, w2_ref, o_ref, buf, sem):
    i = pl.program_id(0)

    @pl.when(i < na_ref[0])
    def _():
        def issue(r, carry):
            t = tok_ref[i * TME + r]
            pltpu.make_async_copy(h2_hbm.at[pl.ds(t, 1)], buf.at[pl.ds(r, 1)], sem).start()
            return carry

        lax.fori_loop(0, TME, issue, 0)
        pltpu.make_async_copy(h2_hbm.at[pl.ds(0, TME)], buf, sem).wait()
        x = buf[...].astype(bf16)
        a = _dot(x, w1_ref[...].astype(bf16))
        b = _dot(x, w3_ref[...].astype(bf16))
        hid = _silu(a) * b
        o_ref[...] = _dot(hid.astype(bf16), w2_ref[...].astype(bf16))

    @pl.when(i >= na_ref[0])
    def _():
        o_ref[...] = jnp.zeros(o_ref.shape, f32)


def _moe(tok_sorted, tile_expert, n_active, h2, w1, w3, w2, layer):
    wspec = lambda shape: pl.BlockSpec((None, None) + shape, lambda i, tok, te, na: (layer, te[i], 0, 0))
    return pl.pallas_call(
        _moe_kernel,
        grid_spec=pltpu.PrefetchScalarGridSpec(
            num_scalar_prefetch=3,
            grid=(N_ETILES,),
            in_specs=[
                pl.BlockSpec(memory_space=pl.ANY),
                wspec((D_MODEL, MOE_D_FF)), wspec((D_MODEL, MOE_D_FF)), wspec((MOE_D_FF, D_MODEL)),
            ],
            out_specs=pl.BlockSpec((TME, D_MODEL), lambda i, tok, te, na: (i, 0)),
            scratch_shapes=[pltpu.VMEM((TME, D_MODEL), f32), pltpu.SemaphoreType.DMA(())],
        ),
        out_shape=jax.ShapeDtypeStruct((P_PAD, D_MODEL), f32),
        compiler_params=_cparams(1),
        name="moe",
    )(tok_sorted, tile_expert, n_active, h2, w1, w3, w2)


def _moe_plan(route):
    experts = route[:, 0:2].astype(i32).reshape(-1)
    onehot = (experts[:, None] == jnp.arange(MOE_EXPERTS, dtype=i32)[None, :]).astype(i32)
    csum = jnp.cumsum(onehot, axis=0)
    rank = jnp.sum(csum * onehot, axis=1) - 1
    counts = csum[-1]
    padded = ((counts + TME - 1) // TME) * TME
    ends = jnp.cumsum(padded)
    pos = (ends - padded)[experts] + rank
    tok_sorted = jnp.zeros((P_PAD,), i32).at[pos].set(jnp.arange(N_PAIRS, dtype=i32) // 2)
    tile_start = jnp.arange(N_ETILES, dtype=i32) * TME
    tile_expert = jnp.minimum(jnp.sum((tile_start[:, None] >= ends[None, :]).astype(i32), axis=1), MOE_EXPERTS - 1)
    n_active = (ends[-1] // TME).reshape(1).astype(i32)
    return pos.astype(i32), tok_sorted, tile_expert.astype(i32), n_active


def _combine_kernel(pos_ref, x1_ref, route_ref, mod_ref, lng_ref, lnb_ref, y_hbm, o_ref, buf, sem):
    i = pl.program_id(0)

    def issue(r, carry):
        p = (i * TM + r) * 2
        pltpu.make_async_copy(y_hbm.at[pl.ds(pos_ref[p], 1)], buf.at[0, pl.ds(r, 1)], sem).start()
        pltpu.make_async_copy(y_hbm.at[pl.ds(pos_ref[p + 1], 1)], buf.at[1, pl.ds(r, 1)], sem).start()
        return carry

    lax.fori_loop(0, TM, issue, 0)
    m = mod_ref[pl.ds(_row_type(i, TM), 1), :]
    g2 = m[:, 5 * D_MODEL:6 * D_MODEL]
    route = route_ref[...]
    for s in range(2):
        pltpu.make_async_copy(y_hbm.at[pl.ds(0, TM)], buf.at[s], sem).wait()
    ff = buf[0] * route[:, 2:3] + buf[1] * route[:, 3:4]
    o_ref[...] = _layer_norm(DEEPNORM_ALPHA * x1_ref[...] + g2 * ff, lng_ref[...], lnb_ref[...])


def _combine(pos, x1, route, mod, ln_g, ln_b, y_sorted):
    const = lambda shape: pl.BlockSpec(shape, lambda i, pos: (0,) * len(shape))
    row = lambda width: pl.BlockSpec((TM, width), lambda i, pos: (i, 0))
    return pl.pallas_call(
        _combine_kernel,
        grid_spec=pltpu.PrefetchScalarGridSpec(
            num_scalar_prefetch=1,
            grid=(T_ALL // TM,),
            in_specs=[row(D_MODEL), row(LANES), const(mod.shape), const(ln_g.shape), const(ln_b.shape),
                      pl.BlockSpec(memory_space=pl.ANY)],
            out_specs=row(D_MODEL),
            scratch_shapes=[pltpu.VMEM((2, TM, D_MODEL), f32), pltpu.SemaphoreType.DMA(())],
        ),
        out_shape=jax.ShapeDtypeStruct((T_ALL, D_MODEL), f32),
        compiler_params=_cparams(1),
        name="combine",
    )(pos, x1, route, mod, ln_g, ln_b, y_sorted)


def _pad_lanes(row):
    return jnp.pad(row, ((0, 0), (0, LANES - row.shape[1])))


def kernel(x_prompt, x_sample, cache_k, cache_v, state_ssm, c, c_ctx, w_ada, b_ada, w_in, na_rpb, ssm_conv_w, ssm_conv_b, ssm_a_log, ssm_dt_bias, ssm_d, ssm_norm_g, gmlp_ln_g, gmlp_ln_b, gmlp_ws, gmlp_bs, w_branch, w_o, ln_g, ln_b, moe_w_coarse, moe_w_fine, moe_w1, moe_w3, moe_w2):
    cvec = jnp.concatenate([c_ctx[None], c, jnp.zeros((N_ROWTYPES - 1 - DEC_BATCH, D_MODEL), f32)], axis=0)
    mods = _mods(cvec, w_ada, b_ada)
    cos, sin = _rope_tables()
    x = jnp.concatenate([x_prompt.reshape(T_CTX, D_MODEL), x_sample.reshape(T_DN, D_MODEL)], axis=0)
    kc = vc = st = None
    for l in range(DEPTH):
        w = w_in[l]
        w_proj = jnp.concatenate([
            w[:, OFF_Q:OFF_XBC + SSM_INNER],
            w[:, OFF_U:OFF_GATE],
            w[:, OFF_XBC + SSM_INNER:OFF_DT],
            jnp.pad(w[:, OFF_DT:OFF_U], ((0, 0), (0, LANES - 2 * SSM_HEADS))),
        ], axis=1).astype(bf16)
        w_gate = w[:, OFF_GATE:].astype(bf16)
        mod = mods[l]
        proj = _inproj(x, mod, w_proj)

        ya, kc, vc = _ctx_attn(proj, l, kc, vc)
        ya = _na_attn(proj, cache_k, cache_v, _na_bias(na_rpb[l]), cos, sin, l, ya)

        ssd_params = (
            ssm_conv_w[l], ssm_conv_b[l][None],
            _pad_lanes(ssm_a_log[l].reshape(1, -1)), _pad_lanes(ssm_dt_bias[l].reshape(1, -1)),
            jnp.repeat(ssm_d[l], SSM_HEAD_DIM)[None], ssm_norm_g[l][None],
        )
        if st is None:
            ys, st = _ssd(proj, ssd_params, l, latent=False)
        else:
            ys, st = _ssd(proj, ssd_params, l, latent=False, st_prev=st)
        (ys,) = _ssd(proj, ssd_params, l, latent=True, h0=state_ssm, y_prev=ys)

        bs_full = jnp.repeat(gmlp_bs[l].T, GMLP_WIDTH // GMLP_GROUPS, axis=1)
        yg = _gmlp(proj, gmlp_ln_g[l][None], gmlp_ln_b[l][None], gmlp_ws[l], bs_full)

        w_r = _pad_lanes(jnp.concatenate([moe_w_coarse[l], moe_w_fine[l]], axis=1))
        x1, h2, route = _merge(x, ya, ys, yg, mod, w_gate, w_branch[l].astype(bf16), w_o[l].astype(bf16),
                               ln_g[l, 0][None], ln_b[l, 0][None], w_r)

        pos, tok_sorted, tile_expert, n_active = _moe_plan(route)
        y_sorted = _moe(tok_sorted, tile_expert, n_active, h2, moe_w1, moe_w3, moe_w2, l)
        x = _combine(pos, x1, route, mod, ln_g[l, 1][None], ln_b[l, 1][None], y_sorted)

    y_prompt = x[:T_CTX].reshape(BATCH, SEQ, D_MODEL)
    y_sample = x[T_CTX:].reshape(DEC_BATCH, DEC_SEQ, D_MODEL)
    return (y_prompt, y_sample, kc, vc, st)
```

```python
import functools
import math

import numpy as np
import jax
import jax.numpy as jnp
from jax import lax
from jax.experimental import pallas as pl
from jax.experimental.pallas import tpu as pltpu

f32 = jnp.float32
bf16 = jnp.bfloat16
i32 = jnp.int32

D_MODEL = 1024
BATCH = 32
SEQ = 256
DEPTH = 2
DEC_BATCH = 2
DEC_SEQ = 1024
PAST_LEN = 256
GRID_W = 64
NA_HEADS = 8
NA_HEAD_DIM = 64
NA_WIDTH = NA_HEADS * NA_HEAD_DIM
NA_WIN_ROWS = 8
NA_WIN_COLS = 16
ROPE_BASE = 10000.0
SSM_HEADS = 8
SSM_HEAD_DIM = 64
SSM_INNER = SSM_HEADS * SSM_HEAD_DIM
SSM_GROUPS = 2
SSM_STATE = 64
SSM_CONV_DIM = SSM_INNER + 2 * SSM_GROUPS * SSM_STATE
CHUNK = 128
GMLP_GROUPS = 4
GMLP_WIDTH = 512
N_BRANCHES = 3
MOE_GROUPS = 4
MOE_EPG = 8
MOE_EXPERTS = MOE_GROUPS * MOE_EPG
MOE_D_FF = 256
DEEPNORM_ALPHA = (2 * DEPTH) ** 0.25
LN_EPS = 1e-5
RMS_EPS = 1e-5

OFF_Q = 0
OFF_Z = 3 * NA_WIDTH
OFF_XBC = OFF_Z + SSM_INNER
OFF_DT = OFF_XBC + SSM_CONV_DIM
OFF_U = OFF_DT + 2 * SSM_HEADS
OFF_GATE = OFF_U + 2 * GMLP_WIDTH
N_IN = OFF_GATE + N_BRANCHES * D_MODEL

LANES = 128
P_Q, P_K, P_V, P_Z, P_X, P_U, P_GV = 0, 512, 1024, 1536, 2048, 2560, 3072
P_B, P_C, P_DT = 3584, 3712, 3840
N_PROJ = 3968

T_CTX = BATCH * SEQ
T_DN = DEC_BATCH * DEC_SEQ
T_ALL = T_CTX + T_DN
N_ROWTYPES = 8

TM = 256
TME = 256
N_PAIRS = 2 * T_ALL
N_ETILES = (N_PAIRS + MOE_EXPERTS * (TME - 1) + TME - 1) // TME
P_PAD = N_ETILES * TME
PLAN_TM = 1024
TE_ROWS = 128
DISP_TM = 512

VMEM_LIMIT = 56 * 1024 * 1024


def _cparams(n_axes):
    return pltpu.CompilerParams(dimension_semantics=("arbitrary",) * n_axes, vmem_limit_bytes=VMEM_LIMIT)


def _dma_cparams():
    return pltpu.CompilerParams(dimension_semantics=("arbitrary",), vmem_limit_bytes=VMEM_LIMIT,
                                disable_bounds_checks=True)


def _row_type(tile, rows_per_tile):
    start = tile * rows_per_tile
    return jnp.where(start < T_CTX, 0, 1 + (start - T_CTX) // DEC_SEQ)


def _silu(x):
    return x * jax.nn.sigmoid(x)


def _gelu(x):
    return 0.5 * x * (1.0 + lax.erf(x * (1.0 / math.sqrt(2.0))))


def _dot(a, b):
    return jnp.dot(a, b, preferred_element_type=f32)


def _dot_nt(a, b):
    return lax.dot_general(a, b, (((1,), (1,)), ((), ())), preferred_element_type=f32)


def _dot_tn(a, b):
    return lax.dot_general(a, b, (((0,), (0,)), ((), ())), preferred_element_type=f32)


def _mods_kernel(c_ref, w_ref, b_ref, o_ref):
    o_ref[...] = _dot(_silu(c_ref[...]).astype(bf16), w_ref[...].astype(bf16)) + b_ref[...]


def _mods(cvec, w_ada, b_ada):
    tn = 1536
    return pl.pallas_call(
        _mods_kernel,
        grid=(DEPTH, 6 * D_MODEL // tn),
        in_specs=[
            pl.BlockSpec((N_ROWTYPES, D_MODEL), lambda l, j: (0, 0)),
            pl.BlockSpec((None, D_MODEL, tn), lambda l, j: (l, 0, j)),
            pl.BlockSpec((None, 1, tn), lambda l, j: (l, 0, j)),
        ],
        out_specs=pl.BlockSpec((None, N_ROWTYPES, tn), lambda l, j: (l, 0, j)),
        out_shape=jax.ShapeDtypeStruct((DEPTH, N_ROWTYPES, 6 * D_MODEL), f32),
        compiler_params=_cparams(2),
        name="mods",
    )(cvec, w_ada, b_ada.reshape(DEPTH, 1, 6 * D_MODEL))


def _inproj_kernel(x_ref, mod_ref, w_ref, o_ref):
    m = mod_ref[pl.ds(_row_type(pl.program_id(0), TM), 1), :]
    h = x_ref[...] * (1.0 + m[:, D_MODEL:2 * D_MODEL]) + m[:, 0:D_MODEL]
    o_ref[...] = _dot(h.astype(bf16), w_ref[...])


def _inproj(x, mod, w_proj):
    return pl.pallas_call(
        _inproj_kernel,
        grid=(T_ALL // TM,),
        in_specs=[
            pl.BlockSpec((TM, D_MODEL), lambda i: (i, 0)),
            pl.BlockSpec((N_ROWTYPES, 6 * D_MODEL), lambda i: (0, 0)),
            pl.BlockSpec((D_MODEL, N_PROJ), lambda i: (0, 0)),
        ],
        out_specs=pl.BlockSpec((TM, N_PROJ), lambda i: (i, 0)),
        out_shape=jax.ShapeDtypeStruct((T_ALL, N_PROJ), f32),
        compiler_params=_cparams(1),
        name="inproj",
    )(x, mod, w_proj)


def _ctx_attn_kernel(q_ref, k_ref, v_ref, y_ref, kc_ref, vc_ref):
    scale = NA_HEAD_DIM ** -0.5
    outs = []
    for h in range(NA_HEADS):
        sl = slice(h * NA_HEAD_DIM, (h + 1) * NA_HEAD_DIM)
        q, k, v = q_ref[:, sl], k_ref[:, sl], v_ref[:, sl]
        kc_ref[h] = k
        vc_ref[h] = v
        s = _dot_nt(q.astype(bf16), k.astype(bf16)) * scale
        e = jnp.exp(s - jnp.max(s, -1, keepdims=True))
        p = e / jnp.sum(e, -1, keepdims=True)
        outs.append(_dot(p.astype(bf16), v.astype(bf16)))
    y_ref[...] = jnp.concatenate(outs, axis=1)


def _ctx_attn(proj, layer, kc_prev, vc_prev):
    cache_shape = jax.ShapeDtypeStruct((BATCH, DEPTH, NA_HEADS, SEQ, NA_HEAD_DIM), f32)
    in_specs = [
        pl.BlockSpec((SEQ, NA_WIDTH), lambda b: (b, P_Q // NA_WIDTH)),
        pl.BlockSpec((SEQ, NA_WIDTH), lambda b: (b, P_K // NA_WIDTH)),
        pl.BlockSpec((SEQ, NA_WIDTH), lambda b: (b, P_V // NA_WIDTH)),
    ]
    args = [proj, proj, proj]
    aliases = {}
    kernel = _ctx_attn_kernel
    if kc_prev is not None:
        in_specs += [pl.BlockSpec(memory_space=pl.ANY)] * 2
        args += [kc_prev, vc_prev]
        aliases = {3: 1, 4: 2}
        kernel = lambda q, k, v, _kc, _vc, y, kc, vc: _ctx_attn_kernel(q, k, v, y, kc, vc)
    cache_spec = pl.BlockSpec((None, None, NA_HEADS, SEQ, NA_HEAD_DIM), lambda b: (b, layer, 0, 0, 0))
    return pl.pallas_call(
        kernel,
        grid=(BATCH,),
        in_specs=in_specs,
        out_specs=[pl.BlockSpec((SEQ, NA_WIDTH), lambda b: (b, 0)), cache_spec, cache_spec],
        out_shape=[jax.ShapeDtypeStruct((T_ALL, NA_WIDTH), f32), cache_shape, cache_shape],
        input_output_aliases=aliases,
        compiler_params=_cparams(1),
        name="ctx_attn",
    )(*args)


GRID_ROWS = DEC_SEQ // GRID_W
N_WIN = NA_WIN_ROWS * GRID_W


def _rope_tables():
    quarter = NA_HEAD_DIM // 4
    freqs = ROPE_BASE ** (-np.arange(quarter, dtype=np.float32) / quarter)
    lane = np.arange(LANES)
    j = lane % NA_HEAD_DIM
    use_col = (j // (NA_HEAD_DIM // 2)) == 1
    fi = j % quarter
    second = (j % (NA_HEAD_DIM // 2)) >= quarter
    pos = jnp.arange(DEC_SEQ)
    p = jnp.where(use_col[None, :], (pos % GRID_W)[:, None], (pos // GRID_W)[:, None]).astype(f32)
    ang = p * jnp.asarray(freqs[fi], f32)[None, :]
    cos = jnp.cos(ang)
    sin = jnp.sin(ang)
    return cos, jnp.where(second[None, :], sin, -sin)


def _na_bias(rpb):
    qc = np.arange(GRID_W)[:, None]
    kc = np.arange(GRID_W)[None, :]
    col0 = np.clip(qc - NA_WIN_COLS // 2, 0, GRID_W - NA_WIN_COLS)
    valid = (kc >= col0) & (kc < col0 + NA_WIN_COLS)
    dc = np.clip(kc - qc + NA_WIN_COLS - 1, 0, 2 * NA_WIN_COLS - 2)
    tb = jnp.where(valid[None, None], rpb[:, :, dc].astype(f32), -jnp.inf)
    rows = [jnp.concatenate([tb[:, d0 + i] for i in range(NA_WIN_ROWS)], axis=-1) for d0 in range(NA_WIN_ROWS)]
    return jnp.stack(rows, axis=1)


def _rope(x, cos, sin_signed):
    lane = lax.broadcasted_iota(i32, x.shape, 1)
    first = (lane % (NA_HEAD_DIM // 2)) < (NA_HEAD_DIM // 4)
    q = NA_HEAD_DIM // 4
    partner = jnp.where(first, pltpu.roll(x, LANES - q, 1), pltpu.roll(x, q, 1))
    return x * cos + partner * sin_signed


def _na_attn_kernel(q_ref, k_ref, v_ref, kctx_ref, vctx_ref, bias_ref, cos_ref, sin_ref, _y_in, y_ref,
                    q_scr, k_scr, v_scr, sc_scr, y_scr):
    scale = NA_HEAD_DIM ** -0.5
    cos, sin = cos_ref[...], sin_ref[...]
    q = _rope(q_ref[...], cos, sin).astype(bf16)
    k = _rope(k_ref[...], cos, sin).astype(bf16)
    v = v_ref[...].astype(bf16)
    for hh in range(2):
        sl = slice(hh * NA_HEAD_DIM, (hh + 1) * NA_HEAD_DIM)
        q_scr[hh] = q[:, sl]
        k_scr[hh] = k[:, sl]
        v_scr[hh] = v[:, sl]
        sc_scr[hh] = _dot_nt(q[:, sl], kctx_ref[hh].astype(bf16)) * scale
    for hh in range(2):
        vctx = vctx_ref[hh].astype(bf16)

        def row_body(r, carry):
            kr0 = jnp.clip(r - NA_WIN_ROWS // 2, 0, GRID_ROWS - NA_WIN_ROWS)
            d0 = kr0 - r + NA_WIN_ROWS - 1
            q0 = pl.multiple_of(r * GRID_W, GRID_W)
            k0 = pl.multiple_of(kr0 * GRID_W, GRID_W)
            qr = q_scr[hh, pl.ds(q0, GRID_W), :]
            kw = k_scr[hh, pl.ds(k0, N_WIN), :]
            vw = v_scr[hh, pl.ds(k0, N_WIN), :]
            s_win = _dot_nt(qr, kw) * scale + bias_ref[hh, d0]
            s_ctx = sc_scr[hh, pl.ds(q0, GRID_W), :]
            m = jnp.maximum(jnp.max(s_win, -1, keepdims=True), jnp.max(s_ctx, -1, keepdims=True))
            e_win = jnp.exp(s_win - m)
            e_ctx = jnp.exp(s_ctx - m)
            denom = jnp.sum(e_win, -1, keepdims=True) + jnp.sum(e_ctx, -1, keepdims=True)
            o = _dot(e_win.astype(bf16), vw) + _dot(e_ctx.astype(bf16), vctx)
            y_scr[hh, pl.ds(q0, GRID_W), :] = o / denom
            return carry

        lax.fori_loop(0, GRID_ROWS, row_body, 0)
    y_ref[...] = jnp.concatenate([y_scr[0], y_scr[1]], axis=1)


def _na_attn(proj, cache_k, cache_v, bias, cos, sin, layer, y_att):
    dn0 = T_CTX // DEC_SEQ
    blk = lambda off: pl.BlockSpec((DEC_SEQ, LANES), lambda b, hp: (dn0 + b, off // LANES + hp))
    ctx_spec = pl.BlockSpec((None, None, 2, PAST_LEN, NA_HEAD_DIM), lambda b, hp: (b, layer, hp, 0, 0))
    return pl.pallas_call(
        _na_attn_kernel,
        grid=(DEC_BATCH, NA_HEADS // 2),
        in_specs=[
            blk(P_Q), blk(P_K), blk(P_V), ctx_spec, ctx_spec,
            pl.BlockSpec((2, NA_WIN_ROWS, GRID_W, N_WIN), lambda b, hp: (hp, 0, 0, 0)),
            pl.BlockSpec((DEC_SEQ, LANES), lambda b, hp: (0, 0)),
            pl.BlockSpec((DEC_SEQ, LANES), lambda b, hp: (0, 0)),
            pl.BlockSpec(memory_space=pl.ANY),
        ],
        out_specs=pl.BlockSpec((DEC_SEQ, LANES), lambda b, hp: (dn0 + b, hp)),
        out_shape=jax.ShapeDtypeStruct((T_ALL, NA_WIDTH), f32),
        scratch_shapes=[
            pltpu.VMEM((2, DEC_SEQ, NA_HEAD_DIM), bf16),
            pltpu.VMEM((2, DEC_SEQ, NA_HEAD_DIM), bf16),
            pltpu.VMEM((2, DEC_SEQ, NA_HEAD_DIM), bf16),
            pltpu.VMEM((2, DEC_SEQ, PAST_LEN), f32),
            pltpu.VMEM((2, DEC_SEQ, NA_HEAD_DIM), f32),
        ],
        input_output_aliases={8: 0},
        compiler_params=_cparams(2),
        name="na_attn",
    )(proj, proj, proj, cache_k, cache_v, bias, cos, sin, y_att)


def _conv_silu(v, w, b):
    n = v.shape[0]
    rows = lax.broadcasted_iota(i32, v.shape, 0)
    prev = jnp.where(rows == 0, 0.0, pltpu.roll(v, 1, 0))
    nxt = jnp.where(rows == n - 1, 0.0, pltpu.roll(v, n - 1, 0))
    return _silu(prev * w[0:1] + v * w[1:2] + nxt * w[2:3] + b)


def _ssd_kernel(*refs, seq_len, has_h0, emit_state, aliased):
    refs = list(refs)
    z_ref, x_ref, b_ref, c_ref, dt_ref, cw_ref, cb_ref, alog_ref, dtb_ref, dskip_ref, ng_ref = refs[:11]
    pos = 11
    h0_ref = None
    if has_h0:
        h0_ref = refs[pos]
        pos += 1
    pos += aliased
    y_ref = refs[pos]
    pos += 1
    st_ref = None
    if emit_state:
        st_ref = refs[pos]
        pos += 1
    xs_scr, b_scr, c_scr, dt_scr, dta_scr, y_scr, h_scr = refs[pos:]

    n_chunks = seq_len // CHUNK
    cw = cw_ref[...]
    cb = cb_ref[...]
    xs = _conv_silu(x_ref[...], cw[:, 0:SSM_INNER], cb[:, 0:SSM_INNER])
    xs_scr[...] = xs
    gn = SSM_GROUPS * SSM_STATE
    b_scr[...] = _conv_silu(b_ref[...], cw[:, SSM_INNER:SSM_INNER + gn], cb[:, SSM_INNER:SSM_INNER + gn])
    c_scr[...] = _conv_silu(c_ref[...], cw[:, SSM_INNER + gn:], cb[:, SSM_INNER + gn:])
    raw = dt_ref[...] + dtb_ref[...]
    dt = jnp.maximum(raw, 0.0) + jnp.log1p(jnp.exp(-jnp.abs(raw)))
    dt_scr[...] = dt
    dta_scr[...] = dt * (-jnp.exp(alog_ref[...]))
    y_scr[...] = xs * dskip_ref[...]
    if has_h0:
        h_scr[...] = h0_ref[...]
    else:
        h_scr[...] = jnp.zeros(h_scr.shape, f32)

    ri = lax.broadcasted_iota(i32, (CHUNK, CHUNK), 0)
    ci = lax.broadcasted_iota(i32, (CHUNK, CHUNK), 1)
    keep = (ri >= ci, ri <= ci)
    rep = SSM_HEADS // SSM_GROUPS

    def one_chunk(direction, chunk):
        r0 = pl.multiple_of(chunk * CHUNK, CHUNK)
        rows = pl.ds(r0, CHUNK)
        mask = keep[direction]
        cum = jnp.dot(mask.astype(f32), dta_scr[rows, :], precision=lax.Precision.HIGHEST,
                      preferred_element_type=f32)
        cum_t = cum.T
        end = cum[CHUNK - 1:CHUNK, :] if direction == 0 else cum[0:1, :]
        grow = jnp.exp(cum)
        to_end = jnp.exp(end - cum)
        chunk_decay = jnp.exp(end)
        bc = b_scr[rows, :].astype(bf16)
        cc = c_scr[rows, :].astype(bf16)
        xc = xs_scr[rows, :]
        dtc = dt_scr[rows, :]
        ys = []
        for g in range(SSM_GROUPS):
            bg = bc[:, g * SSM_STATE:(g + 1) * SSM_STATE]
            cg = cc[:, g * SSM_STATE:(g + 1) * SSM_STATE]
            cb_g = _dot_nt(cg, bg)
            for hh in range(rep):
                h = g * rep + hh
                ln = direction * SSM_HEADS + h
                col = cum[:, ln:ln + 1]
                seg = col - cum_t[ln:ln + 1, :]
                decay = jnp.exp(jnp.where(mask, seg, -jnp.inf))
                xdt = xc[:, h * SSM_HEAD_DIM:(h + 1) * SSM_HEAD_DIM] * dtc[:, ln:ln + 1]
                h_prev = h_scr[direction, h]
                y = _dot((cb_g * decay).astype(bf16), xdt.astype(bf16))
                y = y + _dot_nt(cg, h_prev.astype(bf16)) * grow[:, ln:ln + 1]
                states = _dot_tn((xdt * to_end[:, ln:ln + 1]).astype(bf16), bg)
                h_scr[direction, h] = h_prev * chunk_decay[:, ln:ln + 1] + states
                ys.append(y)
        y_scr[rows, :] = y_scr[rows, :] + jnp.concatenate(ys, axis=1)

    def step(s, carry):
        one_chunk(0, s)
        one_chunk(1, n_chunks - 1 - s)
        return carry

    lax.fori_loop(0, n_chunks, step, 0)

    z = z_ref[...]
    y = y_scr[...] * _silu(z)
    y = y * lax.rsqrt(jnp.mean(jnp.square(y), -1, keepdims=True) + RMS_EPS)
    y_ref[...] = y * ng_ref[...]
    if emit_state:
        st_ref[...] = h_scr[...]


def _ssd(proj, params, layer, *, latent, h0=None, y_prev=None, st_prev=None):
    seq_len = DEC_SEQ if latent else SEQ
    n_seq = DEC_BATCH if latent else BATCH
    blk0 = (T_CTX // DEC_SEQ) if latent else 0
    blk = lambda width, off: pl.BlockSpec((seq_len, width), lambda b: (blk0 + b, off // width))
    const = lambda shape: pl.BlockSpec(shape, lambda b: (0,) * len(shape))
    conv_w, conv_b, a_log_row, dt_bias_row, d_skip_row, norm_g = params
    in_specs = [
        blk(SSM_INNER, P_Z), blk(SSM_INNER, P_X), blk(LANES, P_B), blk(LANES, P_C), blk(LANES, P_DT),
        const(conv_w.shape), const(conv_b.shape), const(a_log_row.shape), const(dt_bias_row.shape),
        const(d_skip_row.shape), const(norm_g.shape),
    ]
    args = [proj, proj, proj, proj, proj, conv_w, conv_b, a_log_row, dt_bias_row, d_skip_row, norm_g]
    state_block = (None, None, 2, SSM_HEADS, SSM_HEAD_DIM, SSM_STATE)
    if h0 is not None:
        in_specs.append(pl.BlockSpec(state_block, lambda b: (b, layer, 0, 0, 0, 0)))
        args.append(h0)
    aliases = {}
    n_alias = 0
    out_specs = [pl.BlockSpec((seq_len, SSM_INNER), lambda b: (blk0 + b, 0))]
    out_shape = [jax.ShapeDtypeStruct((T_ALL, SSM_INNER), f32)]
    if y_prev is not None:
        aliases[len(args)] = 0
        in_specs.append(pl.BlockSpec(memory_space=pl.ANY))
        args.append(y_prev)
        n_alias += 1
    emit_state = not latent
    if emit_state:
        out_specs.append(pl.BlockSpec(state_block, lambda b: (b, layer, 0, 0, 0, 0)))
        out_shape.append(jax.ShapeDtypeStruct((BATCH, DEPTH, 2, SSM_HEADS, SSM_HEAD_DIM, SSM_STATE), f32))
        if st_prev is not None:
            aliases[len(args)] = 1
            in_specs.append(pl.BlockSpec(memory_space=pl.ANY))
            args.append(st_prev)
            n_alias += 1
    return pl.pallas_call(
        functools.partial(_ssd_kernel, seq_len=seq_len, has_h0=h0 is not None, emit_state=emit_state,
                          aliased=n_alias),
        grid=(n_seq,),
        in_specs=in_specs,
        out_specs=out_specs,
        out_shape=out_shape,
        scratch_shapes=[
            pltpu.VMEM((seq_len, SSM_INNER), f32),
            pltpu.VMEM((seq_len, LANES), f32),
            pltpu.VMEM((seq_len, LANES), f32),
            pltpu.VMEM((seq_len, LANES), f32),
            pltpu.VMEM((seq_len, LANES), f32),
            pltpu.VMEM((seq_len, SSM_INNER), f32),
            pltpu.VMEM((2, SSM_HEADS, SSM_HEAD_DIM, SSM_STATE), f32),
        ],
        input_output_aliases=aliases,
        compiler_params=_cparams(1),
        name="ssd_latent" if latent else "ssd_ctx",
    )(*args)


def _gmlp_kernel(u_ref, v_ref, g_ref, b_ref, ws_ref, bs_ref, o_ref):
    u = _gelu(u_ref[...])
    v = _gelu(v_ref[...])
    mu = jnp.mean(v, -1, keepdims=True)
    var = jnp.mean(jnp.square(v - mu), -1, keepdims=True)
    v = ((v - mu) * lax.rsqrt(var + LN_EPS) * g_ref[...] + b_ref[...]).astype(bf16)
    gd = GMLP_WIDTH // GMLP_GROUPS
    rows = []
    for c in range(TM // CHUNK):
        cols = [_dot(ws_ref[g].astype(bf16), v[c * CHUNK:(c + 1) * CHUNK, g * gd:(g + 1) * gd])
                for g in range(GMLP_GROUPS)]
        rows.append(jnp.concatenate(cols, axis=1) + bs_ref[...])
    o_ref[...] = u * jnp.concatenate(rows, axis=0)


def _gmlp(proj, ln_g, ln_b, ws, bs_full):
    const = lambda shape: pl.BlockSpec(shape, lambda i: (0,) * len(shape))
    return pl.pallas_call(
        _gmlp_kernel,
        grid=(T_ALL // TM,),
        in_specs=[
            pl.BlockSpec((TM, GMLP_WIDTH), lambda i: (i, P_U // GMLP_WIDTH)),
            pl.BlockSpec((TM, GMLP_WIDTH), lambda i: (i, P_GV // GMLP_WIDTH)),
            const(ln_g.shape), const(ln_b.shape), const(ws.shape), const(bs_full.shape),
        ],
        out_specs=pl.BlockSpec((TM, GMLP_WIDTH), lambda i: (i, 0)),
        out_shape=jax.ShapeDtypeStruct((T_ALL, GMLP_WIDTH), f32),
        compiler_params=_cparams(1),
        name="gmlp",
    )(proj, proj, ln_g, ln_b, ws, bs_full)


def _layer_norm(x, g, b):
    mu = jnp.mean(x, -1, keepdims=True)
    var = jnp.mean(jnp.square(x - mu), -1, keepdims=True)
    return (x - mu) * lax.rsqrt(var + LN_EPS) * g + b


def _route(logits):
    lane = lax.broadcasted_iota(i32, logits.shape, 1).astype(f32)
    big = 1e9
    lc = jnp.where(lane < MOE_GROUPS, logits, -jnp.inf)
    mc = jnp.max(lc, -1, keepdims=True)
    p_g = 1.0 / jnp.sum(jnp.exp(lc - mc), -1, keepdims=True)
    g_sel = jnp.min(jnp.where(lc == mc, lane, big), -1, keepdims=True)
    lo = MOE_GROUPS + g_sel * MOE_EPG
    fm = jnp.where((lane >= lo) & (lane < lo + MOE_EPG), logits, -jnp.inf)
    v1 = jnp.max(fm, -1, keepdims=True)
    i1 = jnp.min(jnp.where(fm == v1, lane, big), -1, keepdims=True)
    fm2 = jnp.where(lane == i1, -jnp.inf, fm)
    v2 = jnp.max(fm2, -1, keepdims=True)
    i2 = jnp.min(jnp.where(fm2 == v2, lane, big), -1, keepdims=True)
    e2 = jnp.exp(v2 - v1)
    w1 = p_g / (1.0 + e2)
    w2 = p_g * e2 / (1.0 + e2)
    out = jnp.where(lane == 0, i1 - MOE_GROUPS, 0.0)
    out = jnp.where(lane == 1, i2 - MOE_GROUPS, out)
    out = jnp.where(lane == 2, w1, out)
    return jnp.where(lane == 3, w2, out)


def _merge_kernel(x_ref, ya_ref, ys_ref, yg_ref, mod_ref, wg_ref, wb_ref, wo_ref, lng_ref, lnb_ref, wr_ref,
                  x1_ref, h2_ref, route_ref):
    m = mod_ref[pl.ds(_row_type(pl.program_id(0), TM), 1), :]
    sh1, sc1, g1 = m[:, 0:D_MODEL], m[:, D_MODEL:2 * D_MODEL], m[:, 2 * D_MODEL:3 * D_MODEL]
    sh2, sc2 = m[:, 3 * D_MODEL:4 * D_MODEL], m[:, 4 * D_MODEL:5 * D_MODEL]
    x = x_ref[...]
    h = (x * (1.0 + sc1) + sh1).astype(bf16)
    merged = None
    for i, y_ref in enumerate((ya_ref, ys_ref, yg_ref)):
        gate = jax.nn.sigmoid(_dot(h, wg_ref[:, i * D_MODEL:(i + 1) * D_MODEL]))
        term = gate * _dot(y_ref[...].astype(bf16), wb_ref[i])
        merged = term if merged is None else merged + term
    out = _dot(merged.astype(bf16), wo_ref[...])
    x1 = _layer_norm(DEEPNORM_ALPHA * x + g1 * out, lng_ref[...], lnb_ref[...])
    x1_ref[...] = x1
    h2 = x1 * (1.0 + sc2) + sh2
    h2_ref[...] = h2
    logits = jnp.dot(h2, wr_ref[...], precision=lax.Precision.HIGHEST, preferred_element_type=f32)
    route_ref[...] = _route(logits)


def _merge(x, ya, ys, yg, mod, w_gate, w_br, w_o, ln_g, ln_b, w_r):
    const = lambda shape: pl.BlockSpec(shape, lambda i: (0,) * len(shape))
    row = lambda width: pl.BlockSpec((TM, width), lambda i: (i, 0))
    return pl.pallas_call(
        _merge_kernel,
        grid=(T_ALL // TM,),
        in_specs=[
            row(D_MODEL), row(NA_WIDTH), row(SSM_INNER), row(GMLP_WIDTH), const(mod.shape),
            const(w_gate.shape), const(w_br.shape), const(w_o.shape), const(ln_g.shape), const(ln_b.shape),
            const(w_r.shape),
        ],
        out_specs=[row(D_MODEL), row(D_MODEL), row(LANES)],
        out_shape=[
            jax.ShapeDtypeStruct((T_ALL, D_MODEL), f32),
            jax.ShapeDtypeStruct((T_ALL, D_MODEL), f32),
            jax.ShapeDtypeStruct((T_ALL, LANES), f32),
        ],
        compiler_params=_cparams(1),
        name="merge",
    )(x, ya, ys, yg, mod, w_gate, w_br, w_o, ln_g, ln_b, w_r)


def _plan_kernel(route_ref, pos_ref, te_ref, misc_ref, cnt_scr, offs_scr):
    phase, i = pl.program_id(0), pl.program_id(1)
    lane = lax.broadcasted_iota(i32, (PLAN_TM, LANES), 1).astype(f32)
    route = route_ref[...]
    hit0 = lane == route[:, 0:1]
    hit1 = lane == route[:, 1:2]
    onehot = jnp.where(hit0 | hit1, 1.0, 0.0)
    tile_counts = jnp.sum(onehot, axis=0, keepdims=True)

    @pl.when((phase == 0) & (i == 0))
    def _():
        cnt_scr[...] = jnp.zeros(cnt_scr.shape, f32)

    @pl.when(phase == 0)
    def _():
        cnt_scr[...] = cnt_scr[...] + tile_counts

    @pl.when((phase == 1) & (i == 0))
    def _():
        lane1 = lax.broadcasted_iota(i32, (8, LANES), 1)
        counts = jnp.broadcast_to(cnt_scr[...], (8, LANES))
        padded = jnp.where(lane1 < MOE_EXPERTS, jnp.ceil(counts * (1.0 / TME)) * TME, 0.0)
        before = (lax.broadcasted_iota(i32, (LANES, LANES), 0) < lax.broadcasted_iota(i32, (LANES, LANES), 1))
        offs = jnp.dot(padded, before.astype(f32), precision=lax.Precision.HIGHEST, preferred_element_type=f32)
        ends = offs + padded
        offs_scr[...] = offs[0:1]
        cnt_scr[...] = jnp.zeros(cnt_scr.shape, f32)
        tile_start = lax.broadcasted_iota(i32, (TE_ROWS, LANES), 0).astype(f32) * TME
        lane2 = lax.broadcasted_iota(i32, (TE_ROWS, LANES), 1)
        passed = jnp.where((lane2 < MOE_EXPERTS) & (tile_start >= ends[0:1]), 1.0, 0.0)
        te = jnp.minimum(jnp.sum(passed, axis=1, keepdims=True), MOE_EXPERTS - 1.0)
        te_ref[...] = jnp.broadcast_to(te, (TE_ROWS, LANES))
        row = lax.broadcasted_iota(i32, (8, LANES), 0)
        last_tile = jnp.where(padded > 0, ends - TME, -1.0)
        misc_ref[...] = jnp.where(row == 0, ends * (1.0 / TME), jnp.where(row == 1, last_tile, 0.0))

    @pl.when(phase == 1)
    def _():
        base = cnt_scr[...]
        ri = lax.broadcasted_iota(i32, (PLAN_TM, PLAN_TM), 0)
        ci = lax.broadcasted_iota(i32, (PLAN_TM, PLAN_TM), 1)
        earlier = jnp.where(ci < ri, 1.0, 0.0).astype(bf16)
        total = _dot(earlier, onehot.astype(bf16)) + (base + offs_scr[...])
        p0 = jnp.sum(jnp.where(hit0, total, 0.0), axis=1, keepdims=True)
        p1 = jnp.sum(jnp.where(hit1, total, 0.0), axis=1, keepdims=True)
        pos_ref[...] = jnp.where(lane == 0, p0, jnp.where(lane == 1, p1, 0.0))
        cnt_scr[...] = base + tile_counts


def _plan(route):
    pos2, te, misc = pl.pallas_call(
        _plan_kernel,
        grid=(2, T_ALL // PLAN_TM),
        in_specs=[pl.BlockSpec((PLAN_TM, LANES), lambda ph, i: (i, 0))],
        out_specs=[
            pl.BlockSpec((PLAN_TM, LANES), lambda ph, i: (ph * i, 0)),
            pl.BlockSpec((TE_ROWS, LANES), lambda ph, i: (0, 0)),
            pl.BlockSpec((8, LANES), lambda ph, i: (0, 0)),
        ],
        out_shape=[
            jax.ShapeDtypeStruct((T_ALL, LANES), f32),
            jax.ShapeDtypeStruct((TE_ROWS, LANES), f32),
            jax.ShapeDtypeStruct((8, LANES), f32),
        ],
        scratch_shapes=[pltpu.VMEM((1, LANES), f32), pltpu.VMEM((1, LANES), f32)],
        compiler_params=_cparams(2),
        name="plan",
    )(route)
    pos = pos2[:, 0:2].astype(i32).reshape(-1)
    tile_expert = te[:N_ETILES, 0].astype(i32)
    n_active = misc[0, MOE_EXPERTS - 1].astype(i32).reshape(1)
    last_tile = misc[1, :MOE_EXPERTS].astype(i32)
    return pos, tile_expert, n_active, last_tile


def _dispatch_kernel(pos_ref, last_ref, h2_ref, xs_hbm, zbuf, sem):
    i = pl.program_id(0)

    @pl.when(i == 0)
    def _():
        zbuf[...] = jnp.zeros(zbuf.shape, f32)
        for wait in (False, True):
            for e in range(MOE_EXPERTS):
                @pl.when(last_ref[e] >= 0)
                def _():
                    start = pl.multiple_of(last_ref[e], TME)
                    cp = pltpu.make_async_copy(zbuf, xs_hbm.at[pl.ds(start, TME)], sem)
                    cp.wait() if wait else cp.start()

    def issue(r, carry):
        p = (i * DISP_TM + r) * 2
        for k in range(2):
            pltpu.make_async_copy(h2_ref.at[pl.ds(r, 1)], xs_hbm.at[pl.ds(pos_ref[p + k], 1)], sem).start()
        return carry

    lax.fori_loop(0, DISP_TM, issue, 0, unroll=8)
    for k in range(2):
        pltpu.make_async_copy(h2_ref, xs_hbm.at[pl.ds(0, DISP_TM)], sem).wait()


def _dispatch(pos, last_tile, h2):
    return pl.pallas_call(
        _dispatch_kernel,
        grid_spec=pltpu.PrefetchScalarGridSpec(
            num_scalar_prefetch=2,
            grid=(T_ALL // DISP_TM,),
            in_specs=[pl.BlockSpec((DISP_TM, D_MODEL), lambda i, pos, last: (i, 0))],
            out_specs=pl.BlockSpec(memory_space=pl.ANY),
            scratch_shapes=[pltpu.VMEM((TME, D_MODEL), f32), pltpu.SemaphoreType.DMA(())],
        ),
        out_shape=jax.ShapeDtypeStruct((P_PAD, D_MODEL), f32),
        compiler_params=_dma_cparams(),
        name="dispatch",
    )(pos, last_tile, h2)


def _moe_kernel(te_ref, na_ref, x_ref, w1_ref, w3_ref, w2_ref, o_ref, w1_scr, w3_scr, w2_scr):
    i = pl.program_id(0)
    changed = (i == 0) | (te_ref[i] != te_ref[jnp.maximum(i - 1, 0)])

    @pl.when(changed)
    def _():
        w1_scr[...] = w1_ref[...].astype(bf16)
        w3_scr[...] = w3_ref[...].astype(bf16)
        w2_scr[...] = w2_ref[...].astype(bf16)

    @pl.when(i < na_ref[0])
    def _():
        x = x_ref[...].astype(bf16)
        hid = _silu(_dot(x, w1_scr[...])) * _dot(x, w3_scr[...])
        o_ref[...] = _dot(hid.astype(bf16), w2_scr[...])

    @pl.when(i >= na_ref[0])
    def _():
        o_ref[...] = jnp.zeros(o_ref.shape, f32)


def _moe(tile_expert, n_active, x_sorted, w1, w3, w2, layer):
    wspec = lambda shape: pl.BlockSpec((None, None) + shape, lambda i, te, na: (layer, te[i], 0, 0))
    xmap = lambda i, te, na: (jnp.minimum(i, na[0] - 1), 0)
    return pl.pallas_call(
        _moe_kernel,
        grid_spec=pltpu.PrefetchScalarGridSpec(
            num_scalar_prefetch=2,
            grid=(N_ETILES,),
            in_specs=[
                pl.BlockSpec((TME, D_MODEL), xmap),
                wspec((D_MODEL, MOE_D_FF)), wspec((D_MODEL, MOE_D_FF)), wspec((MOE_D_FF, D_MODEL)),
            ],
            out_specs=pl.BlockSpec((TME, D_MODEL), lambda i, te, na: (i, 0)),
            scratch_shapes=[pltpu.VMEM((D_MODEL, MOE_D_FF), bf16), pltpu.VMEM((D_MODEL, MOE_D_FF), bf16),
                            pltpu.VMEM((MOE_D_FF, D_MODEL), bf16)],
        ),
        out_shape=jax.ShapeDtypeStruct((P_PAD, D_MODEL), f32),
        compiler_params=_cparams(1),
        name="moe",
    )(tile_expert, n_active, x_sorted, w1, w3, w2)


def _combine_kernel(pos_ref, x1_ref, route_ref, mod_ref, lng_ref, lnb_ref, y_hbm, o_ref, buf, sem):
    i = pl.program_id(0)

    def issue(r, carry):
        p = (i * TM + r) * 2
        pltpu.make_async_copy(y_hbm.at[pl.ds(pos_ref[p], 1)], buf.at[0, pl.ds(r, 1)], sem).start()
        pltpu.make_async_copy(y_hbm.at[pl.ds(pos_ref[p + 1], 1)], buf.at[1, pl.ds(r, 1)], sem).start()
        return carry

    lax.fori_loop(0, TM, issue, 0, unroll=8)
    m = mod_ref[pl.ds(_row_type(i, TM), 1), :]
    g2 = m[:, 5 * D_MODEL:6 * D_MODEL]
    route = route_ref[...]
    for s in range(2):
        pltpu.make_async_copy(y_hbm.at[pl.ds(0, TM)], buf.at[s], sem).wait()
    ff = buf[0] * route[:, 2:3] + buf[1] * route[:, 3:4]
    o_ref[...] = _layer_norm(DEEPNORM_ALPHA * x1_ref[...] + g2 * ff, lng_ref[...], lnb_ref[...])


def _combine(pos, x1, route, mod, ln_g, ln_b, y_sorted):
    const = lambda shape: pl.BlockSpec(shape, lambda i, pos: (0,) * len(shape))
    row = lambda width: pl.BlockSpec((TM, width), lambda i, pos: (i, 0))
    return pl.pallas_call(
        _combine_kernel,
        grid_spec=pltpu.PrefetchScalarGridSpec(
            num_scalar_prefetch=1,
            grid=(T_ALL // TM,),
            in_specs=[row(D_MODEL), row(LANES), const(mod.shape), const(ln_g.shape), const(ln_b.shape),
                      pl.BlockSpec(memory_space=pl.ANY)],
            out_specs=row(D_MODEL),
            scratch_shapes=[pltpu.VMEM((2, TM, D_MODEL), f32), pltpu.SemaphoreType.DMA(())],
        ),
        out_shape=jax.ShapeDtypeStruct((T_ALL, D_MODEL), f32),
        compiler_params=_dma_cparams(),
        name="combine",
    )(pos, x1, route, mod, ln_g, ln_b, y_sorted)


def _pad_lanes(row):
    return jnp.pad(row, ((0, 0), (0, LANES - row.shape[1])))


def kernel(x_prompt, x_sample, cache_k, cache_v, state_ssm, c, c_ctx, w_ada, b_ada, w_in, na_rpb, ssm_conv_w, ssm_conv_b, ssm_a_log, ssm_dt_bias, ssm_d, ssm_norm_g, gmlp_ln_g, gmlp_ln_b, gmlp_ws, gmlp_bs, w_branch, w_o, ln_g, ln_b, moe_w_coarse, moe_w_fine, moe_w1, moe_w3, moe_w2):
    cvec = jnp.concatenate([c_ctx[None], c, jnp.zeros((N_ROWTYPES - 1 - DEC_BATCH, D_MODEL), f32)], axis=0)
    mods = _mods(cvec, w_ada, b_ada)
    cos, sin = _rope_tables()
    x = jnp.concatenate([x_prompt.reshape(T_CTX, D_MODEL), x_sample.reshape(T_DN, D_MODEL)], axis=0)
    kc = vc = st = None
    for l in range(DEPTH):
        w = w_in[l]
        w_proj = jnp.concatenate([
            w[:, OFF_Q:OFF_XBC + SSM_INNER],
            w[:, OFF_U:OFF_GATE],
            w[:, OFF_XBC + SSM_INNER:OFF_DT],
            jnp.pad(w[:, OFF_DT:OFF_U], ((0, 0), (0, LANES - 2 * SSM_HEADS))),
        ], axis=1).astype(bf16)
        w_gate = w[:, OFF_GATE:].astype(bf16)
        mod = mods[l]
        proj = _inproj(x, mod, w_proj)

        ya, kc, vc = _ctx_attn(proj, l, kc, vc)
        ya = _na_attn(proj, cache_k, cache_v, _na_bias(na_rpb[l]), cos, sin, l, ya)

        ssd_params = (
            ssm_conv_w[l], ssm_conv_b[l][None],
            _pad_lanes(ssm_a_log[l].reshape(1, -1)), _pad_lanes(ssm_dt_bias[l].reshape(1, -1)),
            jnp.repeat(ssm_d[l], SSM_HEAD_DIM)[None], ssm_norm_g[l][None],
        )
        if st is None:
            ys, st = _ssd(proj, ssd_params, l, latent=False)
        else:
            ys, st = _ssd(proj, ssd_params, l, latent=False, st_prev=st)
        (ys,) = _ssd(proj, ssd_params, l, latent=True, h0=state_ssm, y_prev=ys)

        bs_full = jnp.repeat(gmlp_bs[l].T, GMLP_WIDTH // GMLP_GROUPS, axis=1)
        yg = _gmlp(proj, gmlp_ln_g[l][None], gmlp_ln_b[l][None], gmlp_ws[l], bs_full)

        w_r = _pad_lanes(jnp.concatenate([moe_w_coarse[l], moe_w_fine[l]], axis=1))
        x1, h2, route = _merge(x, ya, ys, yg, mod, w_gate, w_branch[l].astype(bf16), w_o[l].astype(bf16),
                               ln_g[l, 0][None], ln_b[l, 0][None], w_r)

        pos, tile_expert, n_active, last_tile = _plan(route)
        x_sorted = _dispatch(pos, last_tile, h2)
        y_sorted = _moe(tile_expert, n_active, x_sorted, moe_w1, moe_w3, moe_w2, l)
        x = _combine(pos, x1, route, mod, ln_g[l, 1][None], ln_b[l, 1][None], y_sorted)

    y_prompt = x[:T_CTX].reshape(BATCH, SEQ, D_MODEL)
    y_sample = x[T_CTX:].reshape(DEC_BATCH, DEC_SEQ, D_MODEL)
    return (y_prompt, y_sample, kc, vc, st)
```

```python
import functools
import math

import numpy as np
import jax
import jax.numpy as jnp
from jax import lax
from jax.experimental import pallas as pl
from jax.experimental.pallas import tpu as pltpu

f32 = jnp.float32
bf16 = jnp.bfloat16
i32 = jnp.int32

D_MODEL = 1024
BATCH = 32
SEQ = 256
DEPTH = 2
DEC_BATCH = 2
DEC_SEQ = 1024
PAST_LEN = 256
GRID_W = 64
NA_HEADS = 8
NA_HEAD_DIM = 64
NA_WIDTH = NA_HEADS * NA_HEAD_DIM
NA_WIN_ROWS = 8
NA_WIN_COLS = 16
ROPE_BASE = 10000.0
SSM_HEADS = 8
SSM_HEAD_DIM = 64
SSM_INNER = SSM_HEADS * SSM_HEAD_DIM
SSM_GROUPS = 2
SSM_STATE = 64
SSM_CONV_DIM = SSM_INNER + 2 * SSM_GROUPS * SSM_STATE
CHUNK = 128
GMLP_GROUPS = 4
GMLP_WIDTH = 512
N_BRANCHES = 3
MOE_GROUPS = 4
MOE_EPG = 8
MOE_EXPERTS = MOE_GROUPS * MOE_EPG
MOE_D_FF = 256
DEEPNORM_ALPHA = (2 * DEPTH) ** 0.25
LN_EPS = 1e-5
RMS_EPS = 1e-5

OFF_Q = 0
OFF_Z = 3 * NA_WIDTH
OFF_XBC = OFF_Z + SSM_INNER
OFF_DT = OFF_XBC + SSM_CONV_DIM
OFF_U = OFF_DT + 2 * SSM_HEADS
OFF_GATE = OFF_U + 2 * GMLP_WIDTH
N_IN = OFF_GATE + N_BRANCHES * D_MODEL

LANES = 128
P_Q, P_K, P_V, P_Z, P_X, P_U, P_GV = 0, 512, 1024, 1536, 2048, 2560, 3072
P_B, P_C, P_DT = 3584, 3712, 3840
N_PROJ = 3968

T_CTX = BATCH * SEQ
T_DN = DEC_BATCH * DEC_SEQ
T_ALL = T_CTX + T_DN
N_ROWTYPES = 8

TM = 256
TME = 256
N_PAIRS = 2 * T_ALL
N_ETILES = (N_PAIRS + MOE_EXPERTS * (TME - 1) + TME - 1) // TME
P_PAD = N_ETILES * TME
PLAN_TM = 1024
TE_ROWS = 128
DISP_TM = 512

VMEM_LIMIT = 56 * 1024 * 1024


def _cparams(n_axes):
    return pltpu.CompilerParams(dimension_semantics=("arbitrary",) * n_axes, vmem_limit_bytes=VMEM_LIMIT)


def _dma_cparams():
    return pltpu.CompilerParams(dimension_semantics=("arbitrary",), vmem_limit_bytes=VMEM_LIMIT,
                                disable_bounds_checks=True)


def _row_type(tile, rows_per_tile):
    start = tile * rows_per_tile
    return jnp.where(start < T_CTX, 0, 1 + (start - T_CTX) // DEC_SEQ)


def _silu(x):
    return x * jax.nn.sigmoid(x)


def _gelu(x):
    return 0.5 * x * (1.0 + lax.erf(x * (1.0 / math.sqrt(2.0))))


def _dot(a, b):
    return jnp.dot(a, b, preferred_element_type=f32)


def _dot_nt(a, b):
    return lax.dot_general(a, b, (((1,), (1,)), ((), ())), preferred_element_type=f32)


def _dot_tn(a, b):
    return lax.dot_general(a, b, (((0,), (0,)), ((), ())), preferred_element_type=f32)


def _mods_kernel(c_ref, w_ref, b_ref, o_ref):
    o_ref[...] = _dot(_silu(c_ref[...]).astype(bf16), w_ref[...].astype(bf16)) + b_ref[...]


def _mods(cvec, w_ada, b_ada):
    tn = 1536
    return pl.pallas_call(
        _mods_kernel,
        grid=(DEPTH, 6 * D_MODEL // tn),
        in_specs=[
            pl.BlockSpec((N_ROWTYPES, D_MODEL), lambda l, j: (0, 0)),
            pl.BlockSpec((None, D_MODEL, tn), lambda l, j: (l, 0, j)),
            pl.BlockSpec((None, 1, tn), lambda l, j: (l, 0, j)),
        ],
        out_specs=pl.BlockSpec((None, N_ROWTYPES, tn), lambda l, j: (l, 0, j)),
        out_shape=jax.ShapeDtypeStruct((DEPTH, N_ROWTYPES, 6 * D_MODEL), f32),
        compiler_params=_cparams(2),
        name="mods",
    )(cvec, w_ada, b_ada.reshape(DEPTH, 1, 6 * D_MODEL))


def _inproj_kernel(x_ref, mod_ref, wh_ref, wt_ref, o_ref, wh_scr):
    @pl.when(pl.program_id(0) == 0)
    def _():
        wh_scr[...] = wh_ref[...].astype(bf16)

    m = mod_ref[pl.ds(_row_type(pl.program_id(0), TM), 1), :]
    h = (x_ref[...] * (1.0 + m[:, D_MODEL:2 * D_MODEL]) + m[:, 0:D_MODEL]).astype(bf16)
    head = _dot(h, wh_scr[...])
    tail = _dot(h, wt_ref[...])
    n_bc = 2 * SSM_GROUPS * SSM_STATE
    o_ref[:, 0:P_U] = head[:, 0:P_U]
    o_ref[:, P_U:P_B] = tail[:, LANES:]
    o_ref[:, P_B:P_B + n_bc] = head[:, P_U:P_U + n_bc]
    o_ref[:, P_DT:N_PROJ] = tail[:, 0:LANES]


def _inproj(x, mod, w_in, w_tail, layer):
    return pl.pallas_call(
        _inproj_kernel,
        grid=(T_ALL // TM,),
        in_specs=[
            pl.BlockSpec((TM, D_MODEL), lambda i: (i, 0)),
            pl.BlockSpec((N_ROWTYPES, 6 * D_MODEL), lambda i: (0, 0)),
            pl.BlockSpec((None, D_MODEL, OFF_DT), lambda i: (layer, 0, 0), pipeline_mode=pl.Buffered(1)),
            pl.BlockSpec(w_tail.shape, lambda i: (0, 0)),
        ],
        out_specs=pl.BlockSpec((TM, N_PROJ), lambda i: (i, 0)),
        out_shape=jax.ShapeDtypeStruct((T_ALL, N_PROJ), f32),
        scratch_shapes=[pltpu.VMEM((D_MODEL, OFF_DT), bf16)],
        compiler_params=_cparams(1),
        name="inproj",
    )(x, mod, w_in, w_tail)


def _ctx_attn_kernel(q_ref, k_ref, v_ref, y_ref, kc_ref, vc_ref):
    scale = NA_HEAD_DIM ** -0.5
    outs = []
    for h in range(NA_HEADS):
        sl = slice(h * NA_HEAD_DIM, (h + 1) * NA_HEAD_DIM)
        q, k, v = q_ref[:, sl], k_ref[:, sl], v_ref[:, sl]
        kc_ref[h] = k
        vc_ref[h] = v
        s = _dot_nt(q.astype(bf16), k.astype(bf16)) * scale
        e = jnp.exp(s - jnp.max(s, -1, keepdims=True))
        p = e / jnp.sum(e, -1, keepdims=True)
        outs.append(_dot(p.astype(bf16), v.astype(bf16)))
    y_ref[...] = jnp.concatenate(outs, axis=1)


def _ctx_attn(proj, layer, kc_prev, vc_prev):
    cache_shape = jax.ShapeDtypeStruct((BATCH, DEPTH, NA_HEADS, SEQ, NA_HEAD_DIM), f32)
    in_specs = [
        pl.BlockSpec((SEQ, NA_WIDTH), lambda b: (b, P_Q // NA_WIDTH)),
        pl.BlockSpec((SEQ, NA_WIDTH), lambda b: (b, P_K // NA_WIDTH)),
        pl.BlockSpec((SEQ, NA_WIDTH), lambda b: (b, P_V // NA_WIDTH)),
    ]
    args = [proj, proj, proj]
    aliases = {}
    kernel = _ctx_attn_kernel
    if kc_prev is not None:
        in_specs += [pl.BlockSpec(memory_space=pl.ANY)] * 2
        args += [kc_prev, vc_prev]
        aliases = {3: 1, 4: 2}
        kernel = lambda q, k, v, _kc, _vc, y, kc, vc: _ctx_attn_kernel(q, k, v, y, kc, vc)
    cache_spec = pl.BlockSpec((None, None, NA_HEADS, SEQ, NA_HEAD_DIM), lambda b: (b, layer, 0, 0, 0))
    return pl.pallas_call(
        kernel,
        grid=(BATCH,),
        in_specs=in_specs,
        out_specs=[pl.BlockSpec((SEQ, NA_WIDTH), lambda b: (b, 0)), cache_spec, cache_spec],
        out_shape=[jax.ShapeDtypeStruct((T_ALL, NA_WIDTH), f32), cache_shape, cache_shape],
        input_output_aliases=aliases,
        compiler_params=_cparams(1),
        name="ctx_attn",
    )(*args)


GRID_ROWS = DEC_SEQ // GRID_W
N_WIN = NA_WIN_ROWS * GRID_W


def _rope_tables():
    quarter = NA_HEAD_DIM // 4
    freqs = ROPE_BASE ** (-np.arange(quarter, dtype=np.float32) / quarter)
    lane = np.arange(LANES)
    j = lane % NA_HEAD_DIM
    use_col = (j // (NA_HEAD_DIM // 2)) == 1
    fi = j % quarter
    second = (j % (NA_HEAD_DIM // 2)) >= quarter
    pos = jnp.arange(DEC_SEQ)
    p = jnp.where(use_col[None, :], (pos % GRID_W)[:, None], (pos // GRID_W)[:, None]).astype(f32)
    ang = p * jnp.asarray(freqs[fi], f32)[None, :]
    cos = jnp.cos(ang)
    sin = jnp.sin(ang)
    return cos, jnp.where(second[None, :], sin, -sin)


def _na_bias(rpb):
    qc = np.arange(GRID_W)[:, None]
    kc = np.arange(GRID_W)[None, :]
    col0 = np.clip(qc - NA_WIN_COLS // 2, 0, GRID_W - NA_WIN_COLS)
    valid = (kc >= col0) & (kc < col0 + NA_WIN_COLS)
    dc = np.clip(kc - qc + NA_WIN_COLS - 1, 0, 2 * NA_WIN_COLS - 2)
    onehot = (dc[None] == np.arange(2 * NA_WIN_COLS - 1)[:, None, None]).astype(np.float32)
    picked = jnp.einsum('hdr,rqk->hdqk', rpb.astype(f32), jnp.asarray(onehot), precision=lax.Precision.HIGHEST)
    tb = jnp.where(valid[None, None], picked, -jnp.inf)
    rows = [jnp.concatenate([tb[:, d0 + i] for i in range(NA_WIN_ROWS)], axis=-1) for d0 in range(NA_WIN_ROWS)]
    return jnp.stack(rows, axis=1)


def _rope(x, cos, sin_signed):
    lane = lax.broadcasted_iota(i32, x.shape, 1)
    first = (lane % (NA_HEAD_DIM // 2)) < (NA_HEAD_DIM // 4)
    q = NA_HEAD_DIM // 4
    partner = jnp.where(first, pltpu.roll(x, LANES - q, 1), pltpu.roll(x, q, 1))
    return x * cos + partner * sin_signed


def _na_attn_kernel(q_ref, k_ref, v_ref, kctx_ref, vctx_ref, bias_ref, cos_ref, sin_ref, _y_in, y_ref,
                    q_scr, k_scr, v_scr, sc_scr, y_scr):
    scale = NA_HEAD_DIM ** -0.5
    cos, sin = cos_ref[...], sin_ref[...]
    q = _rope(q_ref[...], cos, sin).astype(bf16)
    k = _rope(k_ref[...], cos, sin).astype(bf16)
    v = v_ref[...].astype(bf16)
    for hh in range(2):
        sl = slice(hh * NA_HEAD_DIM, (hh + 1) * NA_HEAD_DIM)
        q_scr[hh] = q[:, sl]
        k_scr[hh] = k[:, sl]
        v_scr[hh] = v[:, sl]
        sc_scr[hh] = _dot_nt(q[:, sl], kctx_ref[hh].astype(bf16)) * scale
    for hh in range(2):
        vctx = vctx_ref[hh].astype(bf16)

        def row_body(r, carry):
            kr0 = jnp.clip(r - NA_WIN_ROWS // 2, 0, GRID_ROWS - NA_WIN_ROWS)
            d0 = kr0 - r + NA_WIN_ROWS - 1
            q0 = pl.multiple_of(r * GRID_W, GRID_W)
            k0 = pl.multiple_of(kr0 * GRID_W, GRID_W)
            qr = q_scr[hh, pl.ds(q0, GRID_W), :]
            kw = k_scr[hh, pl.ds(k0, N_WIN), :]
            vw = v_scr[hh, pl.ds(k0, N_WIN), :]
            s_win = _dot_nt(qr, kw) * scale + bias_ref[hh, d0]
            s_ctx = sc_scr[hh, pl.ds(q0, GRID_W), :]
            m = jnp.maximum(jnp.max(s_win, -1, keepdims=True), jnp.max(s_ctx, -1, keepdims=True))
            e_win = jnp.exp(s_win - m)
            e_ctx = jnp.exp(s_ctx - m)
            denom = jnp.sum(e_win, -1, keepdims=True) + jnp.sum(e_ctx, -1, keepdims=True)
            o = _dot(e_win.astype(bf16), vw) + _dot(e_ctx.astype(bf16), vctx)
            y_scr[hh, pl.ds(q0, GRID_W), :] = o / denom
            return carry

        lax.fori_loop(0, GRID_ROWS, row_body, 0)
    y_ref[...] = jnp.concatenate([y_scr[0], y_scr[1]], axis=1)


def _na_attn(proj, cache_k, cache_v, bias, cos, sin, layer, y_att):
    dn0 = T_CTX // DEC_SEQ
    blk = lambda off: pl.BlockSpec((DEC_SEQ, LANES), lambda b, hp: (dn0 + b, off // LANES + hp))
    ctx_spec = pl.BlockSpec((None, None, 2, PAST_LEN, NA_HEAD_DIM), lambda b, hp: (b, layer, hp, 0, 0))
    return pl.pallas_call(
        _na_attn_kernel,
        grid=(DEC_BATCH, NA_HEADS // 2),
        in_specs=[
            blk(P_Q), blk(P_K), blk(P_V), ctx_spec, ctx_spec,
            pl.BlockSpec((2, NA_WIN_ROWS, GRID_W, N_WIN), lambda b, hp: (hp, 0, 0, 0)),
            pl.BlockSpec((DEC_SEQ, LANES), lambda b, hp: (0, 0)),
            pl.BlockSpec((DEC_SEQ, LANES), lambda b, hp: (0, 0)),
            pl.BlockSpec(memory_space=pl.ANY),
        ],
        out_specs=pl.BlockSpec((DEC_SEQ, LANES), lambda b, hp: (dn0 + b, hp)),
        out_shape=jax.ShapeDtypeStruct((T_ALL, NA_WIDTH), f32),
        scratch_shapes=[
            pltpu.VMEM((2, DEC_SEQ, NA_HEAD_DIM), bf16),
            pltpu.VMEM((2, DEC_SEQ, NA_HEAD_DIM), bf16),
            pltpu.VMEM((2, DEC_SEQ, NA_HEAD_DIM), bf16),
            pltpu.VMEM((2, DEC_SEQ, PAST_LEN), f32),
            pltpu.VMEM((2, DEC_SEQ, NA_HEAD_DIM), f32),
        ],
        input_output_aliases={8: 0},
        compiler_params=_cparams(2),
        name="na_attn",
    )(proj, proj, proj, cache_k, cache_v, bias, cos, sin, y_att)


def _conv_silu(v, w, b):
    n = v.shape[0]
    rows = lax.broadcasted_iota(i32, v.shape, 0)
    prev = jnp.where(rows == 0, 0.0, pltpu.roll(v, 1, 0))
    nxt = jnp.where(rows == n - 1, 0.0, pltpu.roll(v, n - 1, 0))
    return _silu(prev * w[0:1] + v * w[1:2] + nxt * w[2:3] + b)


def _ssd_kernel(*refs, seq_len, has_h0, emit_state, aliased):
    refs = list(refs)
    z_ref, x_ref, b_ref, c_ref, dt_ref, cw_ref, cb_ref, alog_ref, dtb_ref, dskip_ref, ng_ref = refs[:11]
    pos = 11
    h0_ref = None
    if has_h0:
        h0_ref = refs[pos]
        pos += 1
    pos += aliased
    y_ref = refs[pos]
    pos += 1
    st_ref = None
    if emit_state:
        st_ref = refs[pos]
        pos += 1
    xs_scr, b_scr, c_scr, dt_scr, dta_scr, y_scr, h_scr = refs[pos:]

    n_chunks = seq_len // CHUNK
    cw = cw_ref[...]
    cb = cb_ref[...]
    xs = _conv_silu(x_ref[...], cw[:, 0:SSM_INNER], cb[:, 0:SSM_INNER])
    xs_scr[...] = xs
    gn = SSM_GROUPS * SSM_STATE
    b_scr[...] = _conv_silu(b_ref[...], cw[:, SSM_INNER:SSM_INNER + gn], cb[:, SSM_INNER:SSM_INNER + gn])
    c_scr[...] = _conv_silu(c_ref[...], cw[:, SSM_INNER + gn:], cb[:, SSM_INNER + gn:])
    raw = dt_ref[...] + dtb_ref[...]
    dt = jnp.maximum(raw, 0.0) + jnp.log1p(jnp.exp(-jnp.abs(raw)))
    dt_scr[...] = dt
    dta_scr[...] = dt * (-jnp.exp(alog_ref[...]))
    y_scr[...] = xs * dskip_ref[...]
    rep = SSM_HEADS // SSM_GROUPS
    head_rows = lambda hh: slice(hh * SSM_HEAD_DIM, (hh + 1) * SSM_HEAD_DIM)
    if has_h0:
        for d in range(2):
            for h in range(SSM_HEADS):
                h_scr[d, h // rep, head_rows(h % rep), :] = h0_ref[d, h]
    else:
        h_scr[...] = jnp.zeros(h_scr.shape, f32)

    ri = lax.broadcasted_iota(i32, (CHUNK, CHUNK), 0)
    ci = lax.broadcasted_iota(i32, (CHUNK, CHUNK), 1)
    keep = (ri >= ci, ri <= ci)
    def one_chunk(direction, chunk):
        r0 = pl.multiple_of(chunk * CHUNK, CHUNK)
        rows = pl.ds(r0, CHUNK)
        mask = keep[direction]
        cum = jnp.dot(mask.astype(f32), dta_scr[rows, :], precision=lax.Precision.HIGHEST,
                      preferred_element_type=f32)
        cum_t = cum.T
        end_row = CHUNK - 1 if direction == 0 else 0
        chunk_decay = jnp.exp(cum[end_row:end_row + 1, :])
        bc = b_scr[rows, :].astype(bf16)
        cc = c_scr[rows, :].astype(bf16)
        xc = xs_scr[rows, :]
        dtc = dt_scr[rows, :]
        gw = rep * SSM_HEAD_DIM
        head_of_lane = lax.broadcasted_iota(i32, (CHUNK, gw), 1) // SSM_HEAD_DIM
        ys = []
        for g in range(SSM_GROUPS):
            bg = bc[:, g * SSM_STATE:(g + 1) * SSM_STATE]
            cg = cc[:, g * SSM_STATE:(g + 1) * SSM_STATE]
            cb_g = _dot_nt(cg, bg)
            scores, cum_x, dt_x, decay_rows = [], None, None, []
            for hh in range(rep):
                ln = direction * SSM_HEADS + g * rep + hh
                col = jnp.broadcast_to(cum[:, ln:ln + 1], (CHUNK, CHUNK))
                decay = jnp.exp(jnp.where(mask, col - cum_t[ln:ln + 1, :], -jnp.inf))
                scores.append((cb_g * decay).astype(bf16))
                col_w = jnp.concatenate([col] * (gw // CHUNK), axis=1)
                dt_w = jnp.broadcast_to(dtc[:, ln:ln + 1], (CHUNK, gw))
                cum_x = col_w if hh == 0 else jnp.where(head_of_lane == hh, col_w, cum_x)
                dt_x = dt_w if hh == 0 else jnp.where(head_of_lane == hh, dt_w, dt_x)
                decay_rows.append(jnp.broadcast_to(chunk_decay[:, ln:ln + 1], (SSM_HEAD_DIM, SSM_STATE)))
            xdt = xc[:, g * gw:(g + 1) * gw] * dt_x
            to_end = jnp.exp(cum_x[end_row:end_row + 1, :] - cum_x)
            rhs = jnp.concatenate([jnp.where(head_of_lane == hh, xdt, 0.0).astype(bf16) for hh in range(rep)], axis=0)
            h_prev = h_scr[direction, g]
            y = _dot(jnp.concatenate(scores, axis=1), rhs)
            y = y + _dot_nt(cg, h_prev.astype(bf16)) * jnp.exp(cum_x)
            states = _dot_tn((xdt * to_end).astype(bf16), bg)
            h_scr[direction, g] = h_prev * jnp.concatenate(decay_rows, axis=0) + states
            ys.append(y)
        y_scr[rows, :] = y_scr[rows, :] + jnp.concatenate(ys, axis=1)

    def step(s, carry):
        one_chunk(0, s)
        one_chunk(1, n_chunks - 1 - s)
        return carry

    lax.fori_loop(0, n_chunks, step, 0)

    z = z_ref[...]
    y = y_scr[...] * _silu(z)
    y = y * lax.rsqrt(jnp.mean(jnp.square(y), -1, keepdims=True) + RMS_EPS)
    y_ref[...] = y * ng_ref[...]
    if emit_state:
        for d in range(2):
            for h in range(SSM_HEADS):
                st_ref[d, h] = h_scr[d, h // rep, head_rows(h % rep), :]


def _ssd(proj, params, layer, *, latent, h0=None, y_prev=None, st_prev=None):
    seq_len = DEC_SEQ if latent else SEQ
    n_seq = DEC_BATCH if latent else BATCH
    blk0 = (T_CTX // DEC_SEQ) if latent else 0
    blk = lambda width, off: pl.BlockSpec((seq_len, width), lambda b: (blk0 + b, off // width))
    const = lambda shape: pl.BlockSpec(shape, lambda b: (0,) * len(shape))
    conv_w, conv_b, a_log_row, dt_bias_row, d_skip_row, norm_g = params
    in_specs = [
        blk(SSM_INNER, P_Z), blk(SSM_INNER, P_X), blk(LANES, P_B), blk(LANES, P_C), blk(LANES, P_DT),
        const(conv_w.shape), const(conv_b.shape), const(a_log_row.shape), const(dt_bias_row.shape),
        const(d_skip_row.shape), const(norm_g.shape),
    ]
    args = [proj, proj, proj, proj, proj, conv_w, conv_b, a_log_row, dt_bias_row, d_skip_row, norm_g]
    state_block = (None, None, 2, SSM_HEADS, SSM_HEAD_DIM, SSM_STATE)
    if h0 is not None:
        in_specs.append(pl.BlockSpec(state_block, lambda b: (b, layer, 0, 0, 0, 0)))
        args.append(h0)
    aliases = {}
    n_alias = 0
    out_specs = [pl.BlockSpec((seq_len, SSM_INNER), lambda b: (blk0 + b, 0))]
    out_shape = [jax.ShapeDtypeStruct((T_ALL, SSM_INNER), f32)]
    if y_prev is not None:
        aliases[len(args)] = 0
        in_specs.append(pl.BlockSpec(memory_space=pl.ANY))
        args.append(y_prev)
        n_alias += 1
    emit_state = not latent
    if emit_state:
        out_specs.append(pl.BlockSpec(state_block, lambda b: (b, layer, 0, 0, 0, 0)))
        out_shape.append(jax.ShapeDtypeStruct((BATCH, DEPTH, 2, SSM_HEADS, SSM_HEAD_DIM, SSM_STATE), f32))
        if st_prev is not None:
            aliases[len(args)] = 1
            in_specs.append(pl.BlockSpec(memory_space=pl.ANY))
            args.append(st_prev)
            n_alias += 1
    return pl.pallas_call(
        functools.partial(_ssd_kernel, seq_len=seq_len, has_h0=h0 is not None, emit_state=emit_state,
                          aliased=n_alias),
        grid=(n_seq,),
        in_specs=in_specs,
        out_specs=out_specs,
        out_shape=out_shape,
        scratch_shapes=[
            pltpu.VMEM((seq_len, SSM_INNER), f32),
            pltpu.VMEM((seq_len, LANES), f32),
            pltpu.VMEM((seq_len, LANES), f32),
            pltpu.VMEM((seq_len, LANES), f32),
            pltpu.VMEM((seq_len, LANES), f32),
            pltpu.VMEM((seq_len, SSM_INNER), f32),
            pltpu.VMEM((2, SSM_GROUPS, SSM_HEADS // SSM_GROUPS * SSM_HEAD_DIM, SSM_STATE), f32),
        ],
        input_output_aliases=aliases,
        compiler_params=_cparams(1),
        name="ssd_latent" if latent else "ssd_ctx",
    )(*args)


def _gmlp_kernel(u_ref, v_ref, g_ref, b_ref, ws_ref, bs_ref, o_ref):
    u = _gelu(u_ref[...])
    v = _gelu(v_ref[...])
    mu = jnp.mean(v, -1, keepdims=True)
    var = jnp.mean(jnp.square(v - mu), -1, keepdims=True)
    v = ((v - mu) * lax.rsqrt(var + LN_EPS) * g_ref[...] + b_ref[...]).astype(bf16)
    gd = GMLP_WIDTH // GMLP_GROUPS
    rows = []
    for c in range(TM // CHUNK):
        cols = [_dot(ws_ref[g].astype(bf16), v[c * CHUNK:(c + 1) * CHUNK, g * gd:(g + 1) * gd])
                for g in range(GMLP_GROUPS)]
        rows.append(jnp.concatenate(cols, axis=1) + bs_ref[...])
    o_ref[...] = u * jnp.concatenate(rows, axis=0)


def _gmlp(proj, ln_g, ln_b, ws, bs_full):
    const = lambda shape: pl.BlockSpec(shape, lambda i: (0,) * len(shape))
    return pl.pallas_call(
        _gmlp_kernel,
        grid=(T_ALL // TM,),
        in_specs=[
            pl.BlockSpec((TM, GMLP_WIDTH), lambda i: (i, P_U // GMLP_WIDTH)),
            pl.BlockSpec((TM, GMLP_WIDTH), lambda i: (i, P_GV // GMLP_WIDTH)),
            const(ln_g.shape), const(ln_b.shape), const(ws.shape), const(bs_full.shape),
        ],
        out_specs=pl.BlockSpec((TM, GMLP_WIDTH), lambda i: (i, 0)),
        out_shape=jax.ShapeDtypeStruct((T_ALL, GMLP_WIDTH), f32),
        compiler_params=_cparams(1),
        name="gmlp",
    )(proj, proj, ln_g, ln_b, ws, bs_full)


def _layer_norm(x, g, b):
    mu = jnp.mean(x, -1, keepdims=True)
    var = jnp.mean(jnp.square(x - mu), -1, keepdims=True)
    return (x - mu) * lax.rsqrt(var + LN_EPS) * g + b


def _route(logits):
    lane = lax.broadcasted_iota(i32, logits.shape, 1).astype(f32)
    big = 1e9
    lc = jnp.where(lane < MOE_GROUPS, logits, -jnp.inf)
    mc = jnp.max(lc, -1, keepdims=True)
    p_g = 1.0 / jnp.sum(jnp.exp(lc - mc), -1, keepdims=True)
    g_sel = jnp.min(jnp.where(lc == mc, lane, big), -1, keepdims=True)
    lo = MOE_GROUPS + g_sel * MOE_EPG
    fm = jnp.where((lane >= lo) & (lane < lo + MOE_EPG), logits, -jnp.inf)
    v1 = jnp.max(fm, -1, keepdims=True)
    i1 = jnp.min(jnp.where(fm == v1, lane, big), -1, keepdims=True)
    fm2 = jnp.where(lane == i1, -jnp.inf, fm)
    v2 = jnp.max(fm2, -1, keepdims=True)
    i2 = jnp.min(jnp.where(fm2 == v2, lane, big), -1, keepdims=True)
    e2 = jnp.exp(v2 - v1)
    w1 = p_g / (1.0 + e2)
    w2 = p_g * e2 / (1.0 + e2)
    out = jnp.where(lane == 0, i1 - MOE_GROUPS, 0.0)
    out = jnp.where(lane == 1, i2 - MOE_GROUPS, out)
    out = jnp.where(lane == 2, w1, out)
    return jnp.where(lane == 3, w2, out)


def _merge_kernel(x_ref, ya_ref, ys_ref, yg_ref, mod_ref, wg_ref, wb_ref, wo_ref, lng_ref, lnb_ref, wr_ref,
                  x1_ref, h2_ref, route_ref):
    m = mod_ref[pl.ds(_row_type(pl.program_id(0), TM), 1), :]
    sh1, sc1, g1 = m[:, 0:D_MODEL], m[:, D_MODEL:2 * D_MODEL], m[:, 2 * D_MODEL:3 * D_MODEL]
    sh2, sc2 = m[:, 3 * D_MODEL:4 * D_MODEL], m[:, 4 * D_MODEL:5 * D_MODEL]
    x = x_ref[...]
    h = (x * (1.0 + sc1) + sh1).astype(bf16)
    merged = None
    for i, y_ref in enumerate((ya_ref, ys_ref, yg_ref)):
        gate = jax.nn.sigmoid(_dot(h, wg_ref[:, i * D_MODEL:(i + 1) * D_MODEL]))
        term = gate * _dot(y_ref[...].astype(bf16), wb_ref[i])
        merged = term if merged is None else merged + term
    out = _dot(merged.astype(bf16), wo_ref[...])
    x1 = _layer_norm(DEEPNORM_ALPHA * x + g1 * out, lng_ref[...], lnb_ref[...])
    x1_ref[...] = x1
    h2 = x1 * (1.0 + sc2) + sh2
    h2_ref[...] = h2
    h2_hi = h2.astype(bf16)
    h2_lo = (h2 - h2_hi.astype(f32)).astype(bf16)
    parts = _dot(h2_hi, wr_ref[...]) + _dot(h2_lo, wr_ref[...])
    logits = parts[:, 0:LANES] + parts[:, LANES:2 * LANES]
    route_ref[...] = _route(logits)


def _merge(x, ya, ys, yg, mod, w_gate, w_br, w_o, ln_g, ln_b, w_r):
    const = lambda shape: pl.BlockSpec(shape, lambda i: (0,) * len(shape))
    row = lambda width: pl.BlockSpec((TM, width), lambda i: (i, 0))
    return pl.pallas_call(
        _merge_kernel,
        grid=(T_ALL // TM,),
        in_specs=[
            row(D_MODEL), row(NA_WIDTH), row(SSM_INNER), row(GMLP_WIDTH), const(mod.shape),
            const(w_gate.shape), const(w_br.shape), const(w_o.shape), const(ln_g.shape), const(ln_b.shape),
            const(w_r.shape),
        ],
        out_specs=[row(D_MODEL), row(D_MODEL), row(LANES)],
        out_shape=[
            jax.ShapeDtypeStruct((T_ALL, D_MODEL), f32),
            jax.ShapeDtypeStruct((T_ALL, D_MODEL), f32),
            jax.ShapeDtypeStruct((T_ALL, LANES), f32),
        ],
        compiler_params=_cparams(1),
        name="merge",
    )(x, ya, ys, yg, mod, w_gate, w_br, w_o, ln_g, ln_b, w_r)


def _plan_kernel(route_ref, pos_ref, te_ref, misc_ref, cnt_scr, offs_scr):
    phase, i = pl.program_id(0), pl.program_id(1)
    lane = lax.broadcasted_iota(i32, (PLAN_TM, LANES), 1).astype(f32)
    route = route_ref[...]
    hit0 = lane == route[:, 0:1]
    hit1 = lane == route[:, 1:2]
    onehot = jnp.where(hit0 | hit1, 1.0, 0.0)
    tile_counts = jnp.sum(onehot, axis=0, keepdims=True)

    @pl.when((phase == 0) & (i == 0))
    def _():
        cnt_scr[...] = jnp.zeros(cnt_scr.shape, f32)

    @pl.when(phase == 0)
    def _():
        cnt_scr[...] = cnt_scr[...] + tile_counts

    @pl.when((phase == 1) & (i == 0))
    def _():
        lane1 = lax.broadcasted_iota(i32, (8, LANES), 1)
        counts = jnp.broadcast_to(cnt_scr[...], (8, LANES))
        padded = jnp.where(lane1 < MOE_EXPERTS, jnp.ceil(counts * (1.0 / TME)) * TME, 0.0)
        before = (lax.broadcasted_iota(i32, (LANES, LANES), 0) < lax.broadcasted_iota(i32, (LANES, LANES), 1))
        offs = jnp.dot(padded, before.astype(f32), precision=lax.Precision.HIGHEST, preferred_element_type=f32)
        ends = offs + padded
        offs_scr[...] = offs[0:1]
        cnt_scr[...] = jnp.zeros(cnt_scr.shape, f32)
        tile_start = lax.broadcasted_iota(i32, (TE_ROWS, LANES), 0).astype(f32) * TME
        lane2 = lax.broadcasted_iota(i32, (TE_ROWS, LANES), 1)
        passed = jnp.where((lane2 < MOE_EXPERTS) & (tile_start >= ends[0:1]), 1.0, 0.0)
        te = jnp.minimum(jnp.sum(passed, axis=1, keepdims=True), MOE_EXPERTS - 1.0)
        te_ref[...] = jnp.broadcast_to(te, (TE_ROWS, LANES))
        row = lax.broadcasted_iota(i32, (8, LANES), 0)
        last_tile = jnp.where(padded > 0, ends - TME, -1.0)
        misc_ref[...] = jnp.where(row == 0, ends * (1.0 / TME), jnp.where(row == 1, last_tile, 0.0))

    @pl.when(phase == 1)
    def _():
        base = cnt_scr[...]
        ri = lax.broadcasted_iota(i32, (PLAN_TM, PLAN_TM), 0)
        ci = lax.broadcasted_iota(i32, (PLAN_TM, PLAN_TM), 1)
        earlier = jnp.where(ci < ri, 1.0, 0.0).astype(bf16)
        total = _dot(earlier, onehot.astype(bf16)) + (base + offs_scr[...])
        p0 = jnp.sum(jnp.where(hit0, total, 0.0), axis=1, keepdims=True)
        p1 = jnp.sum(jnp.where(hit1, total, 0.0), axis=1, keepdims=True)
        pos_ref[...] = jnp.where(lane == 0, p0, jnp.where(lane == 1, p1, 0.0))
        cnt_scr[...] = base + tile_counts


def _plan(route):
    pos2, te, misc = pl.pallas_call(
        _plan_kernel,
        grid=(2, T_ALL // PLAN_TM),
        in_specs=[pl.BlockSpec((PLAN_TM, LANES), lambda ph, i: (i, 0))],
        out_specs=[
            pl.BlockSpec((PLAN_TM, LANES), lambda ph, i: (ph * i, 0)),
            pl.BlockSpec((TE_ROWS, LANES), lambda ph, i: (0, 0)),
            pl.BlockSpec((8, LANES), lambda ph, i: (0, 0)),
        ],
        out_shape=[
            jax.ShapeDtypeStruct((T_ALL, LANES), f32),
            jax.ShapeDtypeStruct((TE_ROWS, LANES), f32),
            jax.ShapeDtypeStruct((8, LANES), f32),
        ],
        scratch_shapes=[pltpu.VMEM((1, LANES), f32), pltpu.VMEM((1, LANES), f32)],
        compiler_params=_cparams(2),
        name="plan",
    )(route)
    pos = pos2[:, 0:2].astype(i32).reshape(-1)
    tile_expert = te[:N_ETILES, 0].astype(i32)
    n_active = misc[0, MOE_EXPERTS - 1].astype(i32).reshape(1)
    last_tile = misc[1, :MOE_EXPERTS].astype(i32)
    return pos, tile_expert, n_active, last_tile


def _dispatch_kernel(pos_ref, last_ref, h2_ref, xs_hbm, zbuf, sem):
    i = pl.program_id(0)

    @pl.when(i == 0)
    def _():
        zbuf[...] = jnp.zeros(zbuf.shape, f32)
        for wait in (False, True):
            for e in range(MOE_EXPERTS):
                @pl.when(last_ref[e] >= 0)
                def _():
                    start = pl.multiple_of(last_ref[e], TME)
                    cp = pltpu.make_async_copy(zbuf, xs_hbm.at[pl.ds(start, TME)], sem)
                    cp.wait() if wait else cp.start()

    def issue(r, carry):
        p = (i * DISP_TM + r) * 2
        for k in range(2):
            pltpu.make_async_copy(h2_ref.at[pl.ds(r, 1)], xs_hbm.at[pl.ds(pos_ref[p + k], 1)], sem).start()
        return carry

    lax.fori_loop(0, DISP_TM, issue, 0, unroll=8)
    for k in range(2):
        pltpu.make_async_copy(h2_ref, xs_hbm.at[pl.ds(0, DISP_TM)], sem).wait()


def _dispatch(pos, last_tile, h2):
    return pl.pallas_call(
        _dispatch_kernel,
        grid_spec=pltpu.PrefetchScalarGridSpec(
            num_scalar_prefetch=2,
            grid=(T_ALL // DISP_TM,),
            in_specs=[pl.BlockSpec((DISP_TM, D_MODEL), lambda i, pos, last: (i, 0))],
            out_specs=pl.BlockSpec(memory_space=pl.ANY),
            scratch_shapes=[pltpu.VMEM((TME, D_MODEL), f32), pltpu.SemaphoreType.DMA(())],
        ),
        out_shape=jax.ShapeDtypeStruct((P_PAD, D_MODEL), f32),
        compiler_params=_dma_cparams(),
        name="dispatch",
    )(pos, last_tile, h2)


def _moe_kernel(te_ref, na_ref, x_ref, w1_ref, w3_ref, w2_ref, o_ref, w1_scr, w3_scr, w2_scr):
    i = pl.program_id(0)
    changed = (i == 0) | (te_ref[i] != te_ref[jnp.maximum(i - 1, 0)])

    @pl.when(changed)
    def _():
        w1_scr[...] = w1_ref[...].astype(bf16)
        w3_scr[...] = w3_ref[...].astype(bf16)
        w2_scr[...] = w2_ref[...].astype(bf16)

    @pl.when(i < na_ref[0])
    def _():
        x = x_ref[...].astype(bf16)
        hid = _silu(_dot(x, w1_scr[...])) * _dot(x, w3_scr[...])
        o_ref[...] = _dot(hid.astype(bf16), w2_scr[...])

    @pl.when(i >= na_ref[0])
    def _():
        o_ref[...] = jnp.zeros(o_ref.shape, f32)


def _moe(tile_expert, n_active, x_sorted, w1, w3, w2, layer):
    wspec = lambda shape: pl.BlockSpec((None, None) + shape, lambda i, te, na: (layer, te[i], 0, 0))
    xmap = lambda i, te, na: (jnp.maximum(jnp.minimum(i, na[0] - 1), 0), 0)
    return pl.pallas_call(
        _moe_kernel,
        grid_spec=pltpu.PrefetchScalarGridSpec(
            num_scalar_prefetch=2,
            grid=(N_ETILES,),
            in_specs=[
                pl.BlockSpec((TME, D_MODEL), xmap),
                wspec((D_MODEL, MOE_D_FF)), wspec((D_MODEL, MOE_D_FF)), wspec((MOE_D_FF, D_MODEL)),
            ],
            out_specs=pl.BlockSpec((TME, D_MODEL), lambda i, te, na: (i, 0)),
            scratch_shapes=[pltpu.VMEM((D_MODEL, MOE_D_FF), bf16), pltpu.VMEM((D_MODEL, MOE_D_FF), bf16),
                            pltpu.VMEM((MOE_D_FF, D_MODEL), bf16)],
        ),
        out_shape=jax.ShapeDtypeStruct((P_PAD, D_MODEL), f32),
        compiler_params=_cparams(1),
        name="moe",
    )(tile_expert, n_active, x_sorted, w1, w3, w2)


def _combine_kernel(pos_ref, x1_ref, route_ref, mod_ref, lng_ref, lnb_ref, y_hbm, o_ref, buf, sem):
    i = pl.program_id(0)

    def issue(r, carry):
        p = (i * TM + r) * 2
        pltpu.make_async_copy(y_hbm.at[pl.ds(pos_ref[p], 1)], buf.at[0, pl.ds(r, 1)], sem).start()
        pltpu.make_async_copy(y_hbm.at[pl.ds(pos_ref[p + 1], 1)], buf.at[1, pl.ds(r, 1)], sem).start()
        return carry

    lax.fori_loop(0, TM, issue, 0, unroll=8)
    m = mod_ref[pl.ds(_row_type(i, TM), 1), :]
    g2 = m[:, 5 * D_MODEL:6 * D_MODEL]
    route = route_ref[...]
    for s in range(2):
        pltpu.make_async_copy(y_hbm.at[pl.ds(0, TM)], buf.at[s], sem).wait()
    ff = buf[0] * route[:, 2:3] + buf[1] * route[:, 3:4]
    o_ref[...] = _layer_norm(DEEPNORM_ALPHA * x1_ref[...] + g2 * ff, lng_ref[...], lnb_ref[...])


def _combine(pos, x1, route, mod, ln_g, ln_b, y_sorted):
    const = lambda shape: pl.BlockSpec(shape, lambda i, pos: (0,) * len(shape))
    row = lambda width: pl.BlockSpec((TM, width), lambda i, pos: (i, 0))
    return pl.pallas_call(
        _combine_kernel,
        grid_spec=pltpu.PrefetchScalarGridSpec(
            num_scalar_prefetch=1,
            grid=(T_ALL // TM,),
            in_specs=[row(D_MODEL), row(LANES), const(mod.shape), const(ln_g.shape), const(ln_b.shape),
                      pl.BlockSpec(memory_space=pl.ANY)],
            out_specs=row(D_MODEL),
            scratch_shapes=[pltpu.VMEM((2, TM, D_MODEL), f32), pltpu.SemaphoreType.DMA(())],
        ),
        out_shape=jax.ShapeDtypeStruct((T_ALL, D_MODEL), f32),
        compiler_params=_dma_cparams(),
        name="combine",
    )(pos, x1, route, mod, ln_g, ln_b, y_sorted)


def _pad_lanes(row):
    return jnp.pad(row, ((0, 0), (0, LANES - row.shape[1])))


def kernel(x_prompt, x_sample, cache_k, cache_v, state_ssm, c, c_ctx, w_ada, b_ada, w_in, na_rpb, ssm_conv_w, ssm_conv_b, ssm_a_log, ssm_dt_bias, ssm_d, ssm_norm_g, gmlp_ln_g, gmlp_ln_b, gmlp_ws, gmlp_bs, w_branch, w_o, ln_g, ln_b, moe_w_coarse, moe_w_fine, moe_w1, moe_w3, moe_w2):
    cvec = jnp.concatenate([c_ctx[None], c, jnp.zeros((N_ROWTYPES - 1 - DEC_BATCH, D_MODEL), f32)], axis=0)
    mods = _mods(cvec, w_ada, b_ada)
    cos, sin = _rope_tables()
    x = jnp.concatenate([x_prompt.reshape(T_CTX, D_MODEL), x_sample.reshape(T_DN, D_MODEL)], axis=0)
    kc = vc = st = None
    for l in range(DEPTH):
        w = w_in[l]
        w_tail = jnp.concatenate([
            jnp.pad(w[:, OFF_DT:OFF_U], ((0, 0), (0, LANES - 2 * SSM_HEADS))),
            w[:, OFF_U:OFF_GATE],
        ], axis=1).astype(bf16)
        w_gate = w[:, OFF_GATE:].astype(bf16)
        mod = mods[l]
        proj = _inproj(x, mod, w_in, w_tail, l)

        ya, kc, vc = _ctx_attn(proj, l, kc, vc)
        ya = _na_attn(proj, cache_k, cache_v, _na_bias(na_rpb[l]), cos, sin, l, ya)

        ssd_params = (
            ssm_conv_w[l], ssm_conv_b[l][None],
            _pad_lanes(ssm_a_log[l].reshape(1, -1)), _pad_lanes(ssm_dt_bias[l].reshape(1, -1)),
            jnp.repeat(ssm_d[l], SSM_HEAD_DIM)[None], ssm_norm_g[l][None],
        )
        if st is None:
            ys, st = _ssd(proj, ssd_params, l, latent=False)
        else:
            ys, st = _ssd(proj, ssd_params, l, latent=False, st_prev=st)
        (ys,) = _ssd(proj, ssd_params, l, latent=True, h0=state_ssm, y_prev=ys)

        bs_full = jnp.repeat(gmlp_bs[l].T, GMLP_WIDTH // GMLP_GROUPS, axis=1)
        yg = _gmlp(proj, gmlp_ln_g[l][None], gmlp_ln_b[l][None], gmlp_ws[l], bs_full)

        w_r = _pad_lanes(jnp.concatenate([moe_w_coarse[l], moe_w_fine[l]], axis=1))
        w_r_hi = w_r.astype(bf16)
        w_r = jnp.concatenate([w_r_hi, (w_r - w_r_hi.astype(f32)).astype(bf16)], axis=1)
        x1, h2, route = _merge(x, ya, ys, yg, mod, w_gate, w_branch[l].astype(bf16), w_o[l].astype(bf16),
                               ln_g[l, 0][None], ln_b[l, 0][None], w_r)

        pos, tile_expert, n_active, last_tile = _plan(route)
        x_sorted = _dispatch(pos, last_tile, h2)
        y_sorted = _moe(tile_expert, n_active, x_sorted, moe_w1, moe_w3, moe_w2, l)
        x = _combine(pos, x1, route, mod, ln_g[l, 1][None], ln_b[l, 1][None], y_sorted)

    y_prompt = x[:T_CTX].reshape(BATCH, SEQ, D_MODEL)
    y_sample = x[T_CTX:].reshape(DEC_BATCH, DEC_SEQ, D_MODEL)
    return (y_prompt, y_sample, kc, vc, st)
```

```python
import functools
import math

import numpy as np
import jax
import jax.numpy as jnp
from jax import lax
from jax.experimental import pallas as pl
from jax.experimental.pallas import tpu as pltpu

f32 = jnp.float32
bf16 = jnp.bfloat16
i32 = jnp.int32

D_MODEL = 1024
BATCH = 32
SEQ = 256
DEPTH = 2
DEC_BATCH = 2
DEC_SEQ = 1024
PAST_LEN = 256
GRID_W = 64
NA_HEADS = 8
NA_HEAD_DIM = 64
NA_WIDTH = NA_HEADS * NA_HEAD_DIM
NA_WIN_ROWS = 8
NA_WIN_COLS = 16
ROPE_BASE = 10000.0
SSM_HEADS = 8
SSM_HEAD_DIM = 64
SSM_INNER = SSM_HEADS * SSM_HEAD_DIM
SSM_GROUPS = 2
SSM_STATE = 64
SSM_CONV_DIM = SSM_INNER + 2 * SSM_GROUPS * SSM_STATE
CHUNK = 128
GMLP_GROUPS = 4
GMLP_WIDTH = 512
N_BRANCHES = 3
MOE_GROUPS = 4
MOE_EPG = 8
MOE_EXPERTS = MOE_GROUPS * MOE_EPG
MOE_D_FF = 256
DEEPNORM_ALPHA = (2 * DEPTH) ** 0.25
LN_EPS = 1e-5
RMS_EPS = 1e-5

OFF_Q = 0
OFF_Z = 3 * NA_WIDTH
OFF_XBC = OFF_Z + SSM_INNER
OFF_DT = OFF_XBC + SSM_CONV_DIM
OFF_U = OFF_DT + 2 * SSM_HEADS
OFF_GATE = OFF_U + 2 * GMLP_WIDTH
N_IN = OFF_GATE + N_BRANCHES * D_MODEL

LANES = 128
P_Q, P_K, P_V, P_Z, P_X, P_U, P_GV = 0, 512, 1024, 1536, 2048, 2560, 3072
P_B, P_C, P_DT = 3584, 3712, 3840
N_PROJ = 3968

T_CTX = BATCH * SEQ
T_DN = DEC_BATCH * DEC_SEQ
T_ALL = T_CTX + T_DN
N_ROWTYPES = 8

TM = 256
TME = 256
N_PAIRS = 2 * T_ALL
N_ETILES = (N_PAIRS + MOE_EXPERTS * (TME - 1) + TME - 1) // TME
P_PAD = N_ETILES * TME
PLAN_TM = 1024
TE_ROWS = 128
DISP_TM = 512

VMEM_LIMIT = 56 * 1024 * 1024


def _cparams(n_axes):
    return pltpu.CompilerParams(dimension_semantics=("arbitrary",) * n_axes, vmem_limit_bytes=VMEM_LIMIT)


def _dma_cparams():
    return pltpu.CompilerParams(dimension_semantics=("arbitrary",), vmem_limit_bytes=VMEM_LIMIT,
                                disable_bounds_checks=True)


def _row_type(tile, rows_per_tile):
    start = tile * rows_per_tile
    return jnp.where(start < T_CTX, 0, 1 + (start - T_CTX) // DEC_SEQ)


def _silu(x):
    return x * jax.nn.sigmoid(x)


def _gelu(x):
    return 0.5 * x * (1.0 + lax.erf(x * (1.0 / math.sqrt(2.0))))


def _dot(a, b):
    return jnp.dot(a, b, preferred_element_type=f32)


def _dot_nt(a, b):
    return lax.dot_general(a, b, (((1,), (1,)), ((), ())), preferred_element_type=f32)


def _dot_tn(a, b):
    return lax.dot_general(a, b, (((0,), (0,)), ((), ())), preferred_element_type=f32)


RT = D_MODEL // LANES


def _rt_shape(n):
    return (n * RT, LANES)


def _rt_rows(ref, start, n=1):
    return ref.at[pl.ds(pl.multiple_of(start * RT, RT), n * RT)]


def _to_row_tiles(ref, value):
    n = value.shape[0]
    for s in range(RT):
        ref[pl.ds(s, n, stride=RT), :] = value[:, s * LANES:(s + 1) * LANES]


def _from_row_tiles(ref):
    n = ref.shape[0] // RT
    return jnp.concatenate([ref[pl.ds(s, n, stride=RT), :] for s in range(RT)], axis=1)


W_PAD = 2 * 3840
GATE_BLOCK = 3840
TAIL_BLOCK = 1408
LANE_SHIFT = OFF_U % LANES
N_CTX_TILES = T_CTX // TM


def _unshift_columns(w, n_out):
    lane = lax.broadcasted_iota(i32, (w.shape[0], LANES), 1)
    rolled = [pltpu.roll(w[:, b * LANES:(b + 1) * LANES], LANES - LANE_SHIFT, 1) for b in range(n_out + 1)]
    return [jnp.where(lane < LANES - LANE_SHIFT, rolled[b], rolled[b + 1]).astype(bf16) for b in range(n_out)]


def _x_specs(imap=lambda i: i):
    ctx = pl.BlockSpec((TM, D_MODEL), lambda i, *_: (jnp.minimum(imap(i), N_CTX_TILES - 1), 0))
    lat = pl.BlockSpec((TM, D_MODEL), lambda i, *_: (jnp.maximum(imap(i) - N_CTX_TILES, 0), 0))
    return [ctx, lat]


def _load_x(xc_ref, xd_ref):
    return jnp.where(pl.program_id(0) < N_CTX_TILES, xc_ref[...], xd_ref[...])


def _mods_kernel(c_ref, w_ref, b_ref, o_ref):
    o_ref[...] = _dot(_silu(c_ref[...]).astype(bf16), w_ref[...].astype(bf16)) + b_ref[...]


def _mods(cvec, w_ada, b_ada):
    tn = 1536
    return pl.pallas_call(
        _mods_kernel,
        grid=(DEPTH, 6 * D_MODEL // tn),
        in_specs=[
            pl.BlockSpec((N_ROWTYPES, D_MODEL), lambda l, j: (0, 0)),
            pl.BlockSpec((None, D_MODEL, tn), lambda l, j: (l, 0, j)),
            pl.BlockSpec((None, 1, tn), lambda l, j: (l, 0, j)),
        ],
        out_specs=pl.BlockSpec((None, N_ROWTYPES, tn), lambda l, j: (l, 0, j)),
        out_shape=jax.ShapeDtypeStruct((DEPTH, N_ROWTYPES, 6 * D_MODEL), f32),
        compiler_params=_cparams(2),
        name="mods",
    )(cvec, w_ada, b_ada.reshape(DEPTH, 1, 6 * D_MODEL))


def _inproj_kernel(xc_ref, xd_ref, mod_ref, wh_ref, wt_ref, o_ref, wh_scr, wt_scr):
    @pl.when(pl.program_id(0) == 0)
    def _():
        wh_scr[...] = wh_ref[...].astype(bf16)
        wt = wt_ref[...]
        lane = lax.broadcasted_iota(i32, (D_MODEL, LANES), 1)
        wt_scr[:, 0:LANES] = jnp.where(lane < LANE_SHIFT, wt[:, 0:LANES], 0.0).astype(bf16)
        for b, blk in enumerate(_unshift_columns(wt, 2 * GMLP_WIDTH // LANES)):
            wt_scr[:, (b + 1) * LANES:(b + 2) * LANES] = blk

    m = mod_ref[pl.ds(_row_type(pl.program_id(0), TM), 1), :]
    h = (_load_x(xc_ref, xd_ref) * (1.0 + m[:, D_MODEL:2 * D_MODEL]) + m[:, 0:D_MODEL]).astype(bf16)
    head = _dot(h, wh_scr[...])
    tail = _dot(h, wt_scr[...])
    n_bc = 2 * SSM_GROUPS * SSM_STATE
    o_ref[:, 0:P_U] = head[:, 0:P_U]
    o_ref[:, P_U:P_B] = tail[:, LANES:]
    o_ref[:, P_B:P_B + n_bc] = head[:, P_U:P_U + n_bc]
    o_ref[:, P_DT:N_PROJ] = tail[:, 0:LANES]


def _inproj(xc, xd, mod, w_pad, layer):
    assert OFF_DT % TAIL_BLOCK == 0 and TAIL_BLOCK >= OFF_GATE - OFF_DT + LANES
    return pl.pallas_call(
        _inproj_kernel,
        grid=(T_ALL // TM,),
        in_specs=_x_specs() + [
            pl.BlockSpec((N_ROWTYPES, 6 * D_MODEL), lambda i: (0, 0)),
            pl.BlockSpec((None, D_MODEL, OFF_DT), lambda i: (layer, 0, 0), pipeline_mode=pl.Buffered(1)),
            pl.BlockSpec((None, D_MODEL, TAIL_BLOCK), lambda i: (layer, 0, OFF_DT // TAIL_BLOCK),
                         pipeline_mode=pl.Buffered(1)),
        ],
        out_specs=pl.BlockSpec((TM, N_PROJ), lambda i: (i, 0)),
        out_shape=jax.ShapeDtypeStruct((T_ALL, N_PROJ), f32),
        scratch_shapes=[pltpu.VMEM((D_MODEL, OFF_DT), bf16), pltpu.VMEM((D_MODEL, LANES + 2 * GMLP_WIDTH), bf16)],
        compiler_params=_cparams(1),
        name="inproj",
    )(xc, xd, mod, w_pad, w_pad)


def _ctx_attn_kernel(q_ref, k_ref, v_ref, y_ref, kc_ref, vc_ref):
    scale = NA_HEAD_DIM ** -0.5
    outs = []
    for h in range(NA_HEADS):
        sl = slice(h * NA_HEAD_DIM, (h + 1) * NA_HEAD_DIM)
        q, k, v = q_ref[:, sl], k_ref[:, sl], v_ref[:, sl]
        kc_ref[h] = k
        vc_ref[h] = v
        s = _dot_nt(q.astype(bf16), k.astype(bf16)) * scale
        e = jnp.exp(s - jnp.max(s, -1, keepdims=True))
        p = e / jnp.sum(e, -1, keepdims=True)
        outs.append(_dot(p.astype(bf16), v.astype(bf16)))
    y_ref[...] = jnp.concatenate(outs, axis=1)


def _ctx_attn(proj, layer, kc_prev, vc_prev):
    cache_shape = jax.ShapeDtypeStruct((BATCH, DEPTH, NA_HEADS, SEQ, NA_HEAD_DIM), f32)
    in_specs = [
        pl.BlockSpec((SEQ, NA_WIDTH), lambda b: (b, P_Q // NA_WIDTH)),
        pl.BlockSpec((SEQ, NA_WIDTH), lambda b: (b, P_K // NA_WIDTH)),
        pl.BlockSpec((SEQ, NA_WIDTH), lambda b: (b, P_V // NA_WIDTH)),
    ]
    args = [proj, proj, proj]
    aliases = {}
    kernel = _ctx_attn_kernel
    if kc_prev is not None:
        in_specs += [pl.BlockSpec(memory_space=pl.ANY)] * 2
        args += [kc_prev, vc_prev]
        aliases = {3: 1, 4: 2}
        kernel = lambda q, k, v, _kc, _vc, y, kc, vc: _ctx_attn_kernel(q, k, v, y, kc, vc)
    cache_spec = pl.BlockSpec((None, None, NA_HEADS, SEQ, NA_HEAD_DIM), lambda b: (b, layer, 0, 0, 0))
    return pl.pallas_call(
        kernel,
        grid=(BATCH,),
        in_specs=in_specs,
        out_specs=[pl.BlockSpec((SEQ, NA_WIDTH), lambda b: (b, 0)), cache_spec, cache_spec],
        out_shape=[jax.ShapeDtypeStruct((T_ALL, NA_WIDTH), f32), cache_shape, cache_shape],
        input_output_aliases=aliases,
        compiler_params=_cparams(1),
        name="ctx_attn",
    )(*args)


GRID_ROWS = DEC_SEQ // GRID_W
N_WIN = NA_WIN_ROWS * GRID_W


def _rope_tables():
    quarter = NA_HEAD_DIM // 4
    freqs = ROPE_BASE ** (-np.arange(quarter, dtype=np.float32) / quarter)
    lane = np.arange(LANES)
    j = lane % NA_HEAD_DIM
    use_col = (j // (NA_HEAD_DIM // 2)) == 1
    fi = j % quarter
    second = (j % (NA_HEAD_DIM // 2)) >= quarter
    pos = jnp.arange(DEC_SEQ)
    p = jnp.where(use_col[None, :], (pos % GRID_W)[:, None], (pos // GRID_W)[:, None]).astype(f32)
    ang = p * jnp.asarray(freqs[fi], f32)[None, :]
    cos = jnp.cos(ang)
    sin = jnp.sin(ang)
    return cos, jnp.where(second[None, :], sin, -sin)


def _na_bias(rpb):
    qc = np.arange(GRID_W)[:, None]
    kc = np.arange(GRID_W)[None, :]
    col0 = np.clip(qc - NA_WIN_COLS // 2, 0, GRID_W - NA_WIN_COLS)
    valid = (kc >= col0) & (kc < col0 + NA_WIN_COLS)
    dc = np.clip(kc - qc + NA_WIN_COLS - 1, 0, 2 * NA_WIN_COLS - 2)
    onehot = (dc[None] == np.arange(2 * NA_WIN_COLS - 1)[:, None, None]).astype(np.float32)
    picked = jnp.einsum('hdr,rqk->hdqk', rpb.astype(f32), jnp.asarray(onehot), precision=lax.Precision.HIGHEST)
    tb = jnp.where(valid[None, None], picked, -jnp.inf)
    rows = [jnp.concatenate([tb[:, d0 + i] for i in range(NA_WIN_ROWS)], axis=-1) for d0 in range(NA_WIN_ROWS)]
    return jnp.stack(rows, axis=1)


def _rope(x, cos, sin_signed):
    lane = lax.broadcasted_iota(i32, x.shape, 1)
    first = (lane % (NA_HEAD_DIM // 2)) < (NA_HEAD_DIM // 4)
    q = NA_HEAD_DIM // 4
    partner = jnp.where(first, pltpu.roll(x, LANES - q, 1), pltpu.roll(x, q, 1))
    return x * cos + partner * sin_signed


def _na_attn_kernel(q_ref, k_ref, v_ref, kctx_ref, vctx_ref, bias_ref, cos_ref, sin_ref, _y_in, y_ref,
                    q_scr, k_scr, v_scr, sc_scr, y_scr):
    scale = NA_HEAD_DIM ** -0.5
    cos, sin = cos_ref[...], sin_ref[...]
    q = _rope(q_ref[...], cos, sin).astype(bf16)
    k = _rope(k_ref[...], cos, sin).astype(bf16)
    v = v_ref[...].astype(bf16)
    for hh in range(2):
        sl = slice(hh * NA_HEAD_DIM, (hh + 1) * NA_HEAD_DIM)
        q_scr[hh] = q[:, sl]
        k_scr[hh] = k[:, sl]
        v_scr[hh] = v[:, sl]
        sc_scr[hh] = _dot_nt(q[:, sl], kctx_ref[hh].astype(bf16)) * scale
    for hh in range(2):
        vctx = vctx_ref[hh].astype(bf16)

        def row_body(r, carry):
            kr0 = jnp.clip(r - NA_WIN_ROWS // 2, 0, GRID_ROWS - NA_WIN_ROWS)
            d0 = kr0 - r + NA_WIN_ROWS - 1
            q0 = pl.multiple_of(r * GRID_W, GRID_W)
            k0 = pl.multiple_of(kr0 * GRID_W, GRID_W)
            qr = q_scr[hh, pl.ds(q0, GRID_W), :]
            kw = k_scr[hh, pl.ds(k0, N_WIN), :]
            vw = v_scr[hh, pl.ds(k0, N_WIN), :]
            s_win = _dot_nt(qr, kw) * scale + bias_ref[hh, d0]
            s_ctx = sc_scr[hh, pl.ds(q0, GRID_W), :]
            m = jnp.maximum(jnp.max(s_win, -1, keepdims=True), jnp.max(s_ctx, -1, keepdims=True))
            e_win = jnp.exp(s_win - m)
            e_ctx = jnp.exp(s_ctx - m)
            denom = jnp.sum(e_win, -1, keepdims=True) + jnp.sum(e_ctx, -1, keepdims=True)
            o = _dot(e_win.astype(bf16), vw) + _dot(e_ctx.astype(bf16), vctx)
            y_scr[hh, pl.ds(q0, GRID_W), :] = o / denom
            return carry

        lax.fori_loop(0, GRID_ROWS, row_body, 0)
    y_ref[...] = jnp.concatenate([y_scr[0], y_scr[1]], axis=1)


def _na_attn(proj, cache_k, cache_v, bias, cos, sin, layer, y_att):
    dn0 = T_CTX // DEC_SEQ
    blk = lambda off: pl.BlockSpec((DEC_SEQ, LANES), lambda b, hp: (dn0 + b, off // LANES + hp))
    ctx_spec = pl.BlockSpec((None, None, 2, PAST_LEN, NA_HEAD_DIM), lambda b, hp: (b, layer, hp, 0, 0))
    return pl.pallas_call(
        _na_attn_kernel,
        grid=(DEC_BATCH, NA_HEADS // 2),
        in_specs=[
            blk(P_Q), blk(P_K), blk(P_V), ctx_spec, ctx_spec,
            pl.BlockSpec((2, NA_WIN_ROWS, GRID_W, N_WIN), lambda b, hp: (hp, 0, 0, 0)),
            pl.BlockSpec((DEC_SEQ, LANES), lambda b, hp: (0, 0)),
            pl.BlockSpec((DEC_SEQ, LANES), lambda b, hp: (0, 0)),
            pl.BlockSpec(memory_space=pl.ANY),
        ],
        out_specs=pl.BlockSpec((DEC_SEQ, LANES), lambda b, hp: (dn0 + b, hp)),
        out_shape=jax.ShapeDtypeStruct((T_ALL, NA_WIDTH), f32),
        scratch_shapes=[
            pltpu.VMEM((2, DEC_SEQ, NA_HEAD_DIM), bf16),
            pltpu.VMEM((2, DEC_SEQ, NA_HEAD_DIM), bf16),
            pltpu.VMEM((2, DEC_SEQ, NA_HEAD_DIM), bf16),
            pltpu.VMEM((2, DEC_SEQ, PAST_LEN), f32),
            pltpu.VMEM((2, DEC_SEQ, NA_HEAD_DIM), f32),
        ],
        input_output_aliases={8: 0},
        compiler_params=_cparams(2),
        name="na_attn",
    )(proj, proj, proj, cache_k, cache_v, bias, cos, sin, y_att)


def _conv_silu(v, w, b):
    n = v.shape[0]
    rows = lax.broadcasted_iota(i32, v.shape, 0)
    prev = jnp.where(rows == 0, 0.0, pltpu.roll(v, 1, 0))
    nxt = jnp.where(rows == n - 1, 0.0, pltpu.roll(v, n - 1, 0))
    return _silu(prev * w[0:1] + v * w[1:2] + nxt * w[2:3] + b)


def _ssd_kernel(*refs, seq_len, has_h0, emit_state, aliased):
    refs = list(refs)
    z_ref, x_ref, b_ref, c_ref, dt_ref, cw_ref, cb_ref, alog_ref, dtb_ref, dskip_ref, ng_ref = refs[:11]
    pos = 11
    h0_ref = None
    if has_h0:
        h0_ref = refs[pos]
        pos += 1
    pos += aliased
    y_ref = refs[pos]
    pos += 1
    st_ref = None
    if emit_state:
        st_ref = refs[pos]
        pos += 1
    xs_scr, b_scr, c_scr, dt_scr, dta_scr, y_scr, h_scr = refs[pos:]

    n_chunks = seq_len // CHUNK
    cw = cw_ref[...]
    cb = cb_ref[...]
    xs = _conv_silu(x_ref[...], cw[:, 0:SSM_INNER], cb[:, 0:SSM_INNER])
    xs_scr[...] = xs
    gn = SSM_GROUPS * SSM_STATE
    b_scr[...] = _conv_silu(b_ref[...], cw[:, SSM_INNER:SSM_INNER + gn], cb[:, SSM_INNER:SSM_INNER + gn])
    c_scr[...] = _conv_silu(c_ref[...], cw[:, SSM_INNER + gn:], cb[:, SSM_INNER + gn:])
    raw = dt_ref[...] + dtb_ref[...]
    dt = jnp.maximum(raw, 0.0) + jnp.log1p(jnp.exp(-jnp.abs(raw)))
    dt_scr[...] = dt
    dta_scr[...] = dt * (-jnp.exp(alog_ref[...]))
    y_scr[...] = xs * dskip_ref[...]
    rep = SSM_HEADS // SSM_GROUPS
    head_rows = lambda hh: slice(hh * SSM_HEAD_DIM, (hh + 1) * SSM_HEAD_DIM)
    if has_h0:
        for d in range(2):
            for h in range(SSM_HEADS):
                h_scr[d, h // rep, head_rows(h % rep), :] = h0_ref[d, h]
    else:
        h_scr[...] = jnp.zeros(h_scr.shape, f32)

    ri = lax.broadcasted_iota(i32, (CHUNK, CHUNK), 0)
    ci = lax.broadcasted_iota(i32, (CHUNK, CHUNK), 1)
    keep = (ri >= ci, ri <= ci)
    def one_chunk(direction, chunk):
        r0 = pl.multiple_of(chunk * CHUNK, CHUNK)
        rows = pl.ds(r0, CHUNK)
        mask = keep[direction]
        cum = jnp.dot(mask.astype(f32), dta_scr[rows, :], precision=lax.Precision.HIGHEST,
                      preferred_element_type=f32)
        cum_t = cum.T
        end_row = CHUNK - 1 if direction == 0 else 0
        chunk_decay = jnp.exp(cum[end_row:end_row + 1, :])
        bc = b_scr[rows, :].astype(bf16)
        cc = c_scr[rows, :].astype(bf16)
        xc = xs_scr[rows, :]
        dtc = dt_scr[rows, :]
        gw = rep * SSM_HEAD_DIM
        head_of_lane = lax.broadcasted_iota(i32, (CHUNK, gw), 1) // SSM_HEAD_DIM
        ys = []
        for g in range(SSM_GROUPS):
            bg = bc[:, g * SSM_STATE:(g + 1) * SSM_STATE]
            cg = cc[:, g * SSM_STATE:(g + 1) * SSM_STATE]
            cb_g = _dot_nt(cg, bg)
            scores, cum_x, dt_x, decay_rows = [], None, None, []
            for hh in range(rep):
                ln = direction * SSM_HEADS + g * rep + hh
                col = jnp.broadcast_to(cum[:, ln:ln + 1], (CHUNK, CHUNK))
                decay = jnp.exp(jnp.where(mask, col - cum_t[ln:ln + 1, :], -jnp.inf))
                scores.append((cb_g * decay).astype(bf16))
                col_w = jnp.concatenate([col] * (gw // CHUNK), axis=1)
                dt_w = jnp.broadcast_to(dtc[:, ln:ln + 1], (CHUNK, gw))
                cum_x = col_w if hh == 0 else jnp.where(head_of_lane == hh, col_w, cum_x)
                dt_x = dt_w if hh == 0 else jnp.where(head_of_lane == hh, dt_w, dt_x)
                decay_rows.append(jnp.broadcast_to(chunk_decay[:, ln:ln + 1], (SSM_HEAD_DIM, SSM_STATE)))
            xdt = xc[:, g * gw:(g + 1) * gw] * dt_x
            to_end = jnp.exp(cum_x[end_row:end_row + 1, :] - cum_x)
            rhs = jnp.concatenate([jnp.where(head_of_lane == hh, xdt, 0.0).astype(bf16) for hh in range(rep)], axis=0)
            h_prev = h_scr[direction, g]
            y = _dot(jnp.concatenate(scores, axis=1), rhs)
            y = y + _dot_nt(cg, h_prev.astype(bf16)) * jnp.exp(cum_x)
            states = _dot_tn((xdt * to_end).astype(bf16), bg)
            h_scr[direction, g] = h_prev * jnp.concatenate(decay_rows, axis=0) + states
            ys.append(y)
        y_scr[rows, :] = y_scr[rows, :] + jnp.concatenate(ys, axis=1)

    def step(s, carry):
        one_chunk(0, s)
        one_chunk(1, n_chunks - 1 - s)
        return carry

    lax.fori_loop(0, n_chunks, step, 0)

    z = z_ref[...]
    y = y_scr[...] * _silu(z)
    y = y * lax.rsqrt(jnp.mean(jnp.square(y), -1, keepdims=True) + RMS_EPS)
    y_ref[...] = y * ng_ref[...]
    if emit_state:
        for d in range(2):
            for h in range(SSM_HEADS):
                st_ref[d, h] = h_scr[d, h // rep, head_rows(h % rep), :]


def _ssd(proj, params, layer, *, latent, h0=None, y_prev=None, st_prev=None):
    seq_len = DEC_SEQ if latent else SEQ
    n_seq = DEC_BATCH if latent else BATCH
    blk0 = (T_CTX // DEC_SEQ) if latent else 0
    blk = lambda width, off: pl.BlockSpec((seq_len, width), lambda b: (blk0 + b, off // width))
    const = lambda shape: pl.BlockSpec(shape, lambda b: (0,) * len(shape))
    conv_w, conv_b, a_log_row, dt_bias_row, d_skip_row, norm_g = params
    in_specs = [
        blk(SSM_INNER, P_Z), blk(SSM_INNER, P_X), blk(LANES, P_B), blk(LANES, P_C), blk(LANES, P_DT),
        const(conv_w.shape), const(conv_b.shape), const(a_log_row.shape), const(dt_bias_row.shape),
        const(d_skip_row.shape), const(norm_g.shape),
    ]
    args = [proj, proj, proj, proj, proj, conv_w, conv_b, a_log_row, dt_bias_row, d_skip_row, norm_g]
    state_block = (None, None, 2, SSM_HEADS, SSM_HEAD_DIM, SSM_STATE)
    if h0 is not None:
        in_specs.append(pl.BlockSpec(state_block, lambda b: (b, layer, 0, 0, 0, 0)))
        args.append(h0)
    aliases = {}
    n_alias = 0
    out_specs = [pl.BlockSpec((seq_len, SSM_INNER), lambda b: (blk0 + b, 0))]
    out_shape = [jax.ShapeDtypeStruct((T_ALL, SSM_INNER), f32)]
    if y_prev is not None:
        aliases[len(args)] = 0
        in_specs.append(pl.BlockSpec(memory_space=pl.ANY))
        args.append(y_prev)
        n_alias += 1
    emit_state = not latent
    if emit_state:
        out_specs.append(pl.BlockSpec(state_block, lambda b: (b, layer, 0, 0, 0, 0)))
        out_shape.append(jax.ShapeDtypeStruct((BATCH, DEPTH, 2, SSM_HEADS, SSM_HEAD_DIM, SSM_STATE), f32))
        if st_prev is not None:
            aliases[len(args)] = 1
            in_specs.append(pl.BlockSpec(memory_space=pl.ANY))
            args.append(st_prev)
            n_alias += 1
    return pl.pallas_call(
        functools.partial(_ssd_kernel, seq_len=seq_len, has_h0=h0 is not None, emit_state=emit_state,
                          aliased=n_alias),
        grid=(n_seq,),
        in_specs=in_specs,
        out_specs=out_specs,
        out_shape=out_shape,
        scratch_shapes=[
            pltpu.VMEM((seq_len, SSM_INNER), f32),
            pltpu.VMEM((seq_len, LANES), f32),
            pltpu.VMEM((seq_len, LANES), f32),
            pltpu.VMEM((seq_len, LANES), f32),
            pltpu.VMEM((seq_len, LANES), f32),
            pltpu.VMEM((seq_len, SSM_INNER), f32),
            pltpu.VMEM((2, SSM_GROUPS, SSM_HEADS // SSM_GROUPS * SSM_HEAD_DIM, SSM_STATE), f32),
        ],
        input_output_aliases=aliases,
        compiler_params=_cparams(1),
        name="ssd_latent" if latent else "ssd_ctx",
    )(*args)


def _gmlp_kernel(u_ref, v_ref, g_ref, b_ref, ws_ref, bs_ref, o_ref):
    u = _gelu(u_ref[...])
    v = _gelu(v_ref[...])
    mu = jnp.mean(v, -1, keepdims=True)
    var = jnp.mean(jnp.square(v - mu), -1, keepdims=True)
    v = ((v - mu) * lax.rsqrt(var + LN_EPS) * g_ref[...] + b_ref[...]).astype(bf16)
    gd = GMLP_WIDTH // GMLP_GROUPS
    rows = []
    for c in range(TM // CHUNK):
        cols = [_dot(ws_ref[g].astype(bf16), v[c * CHUNK:(c + 1) * CHUNK, g * gd:(g + 1) * gd])
                for g in range(GMLP_GROUPS)]
        rows.append(jnp.concatenate(cols, axis=1) + bs_ref[...])
    o_ref[...] = u * jnp.concatenate(rows, axis=0)


def _gmlp(proj, ln_g, ln_b, ws, bs_full):
    const = lambda shape: pl.BlockSpec(shape, lambda i: (0,) * len(shape))
    return pl.pallas_call(
        _gmlp_kernel,
        grid=(T_ALL // TM,),
        in_specs=[
            pl.BlockSpec((TM, GMLP_WIDTH), lambda i: (i, P_U // GMLP_WIDTH)),
            pl.BlockSpec((TM, GMLP_WIDTH), lambda i: (i, P_GV // GMLP_WIDTH)),
            const(ln_g.shape), const(ln_b.shape), const(ws.shape), const(bs_full.shape),
        ],
        out_specs=pl.BlockSpec((TM, GMLP_WIDTH), lambda i: (i, 0)),
        out_shape=jax.ShapeDtypeStruct((T_ALL, GMLP_WIDTH), f32),
        compiler_params=_cparams(1),
        name="gmlp",
    )(proj, proj, ln_g, ln_b, ws, bs_full)


def _layer_norm(x, g, b):
    mu = jnp.mean(x, -1, keepdims=True)
    var = jnp.mean(jnp.square(x - mu), -1, keepdims=True)
    return (x - mu) * lax.rsqrt(var + LN_EPS) * g + b


def _route(logits):
    lane = lax.broadcasted_iota(i32, logits.shape, 1).astype(f32)
    big = 1e9
    lc = jnp.where(lane < MOE_GROUPS, logits, -jnp.inf)
    mc = jnp.max(lc, -1, keepdims=True)
    p_g = 1.0 / jnp.sum(jnp.exp(lc - mc), -1, keepdims=True)
    g_sel = jnp.min(jnp.where(lc == mc, lane, big), -1, keepdims=True)
    lo = MOE_GROUPS + g_sel * MOE_EPG
    fm = jnp.where((lane >= lo) & (lane < lo + MOE_EPG), logits, -jnp.inf)
    v1 = jnp.max(fm, -1, keepdims=True)
    i1 = jnp.min(jnp.where(fm == v1, lane, big), -1, keepdims=True)
    fm2 = jnp.where(lane == i1, -jnp.inf, fm)
    v2 = jnp.max(fm2, -1, keepdims=True)
    i2 = jnp.min(jnp.where(fm2 == v2, lane, big), -1, keepdims=True)
    e2 = jnp.exp(v2 - v1)
    w1 = p_g / (1.0 + e2)
    w2 = p_g * e2 / (1.0 + e2)
    out = jnp.where(lane == 0, i1 - MOE_GROUPS, 0.0)
    out = jnp.where(lane == 1, i2 - MOE_GROUPS, out)
    out = jnp.where(lane == 2, w1, out)
    return jnp.where(lane == 3, w2, out)


def _merge_kernel(xc_ref, xd_ref, ya_ref, ys_ref, yg_ref, mod_ref, wg_ref, wb_ref, wo_ref, lng_ref, lnb_ref, wr_ref,
                  x1_ref, h2_ref, route_ref, wg_scr):
    @pl.when(pl.program_id(0) == 0)
    def _():
        for b, blk in enumerate(_unshift_columns(wg_ref[...], N_BRANCHES * D_MODEL // LANES)):
            wg_scr[:, b * LANES:(b + 1) * LANES] = blk

    m = mod_ref[pl.ds(_row_type(pl.program_id(0), TM), 1), :]
    sh1, sc1, g1 = m[:, 0:D_MODEL], m[:, D_MODEL:2 * D_MODEL], m[:, 2 * D_MODEL:3 * D_MODEL]
    sh2, sc2 = m[:, 3 * D_MODEL:4 * D_MODEL], m[:, 4 * D_MODEL:5 * D_MODEL]
    x = _load_x(xc_ref, xd_ref)
    h = (x * (1.0 + sc1) + sh1).astype(bf16)
    merged = None
    for i, y_ref in enumerate((ya_ref, ys_ref, yg_ref)):
        gate = jax.nn.sigmoid(_dot(h, wg_scr[:, i * D_MODEL:(i + 1) * D_MODEL]))
        term = gate * _dot(y_ref[...].astype(bf16), wb_ref[i])
        merged = term if merged is None else merged + term
    out = _dot(merged.astype(bf16), wo_ref[...])
    x1 = _layer_norm(DEEPNORM_ALPHA * x + g1 * out, lng_ref[...], lnb_ref[...])
    x1_ref[...] = x1
    h2 = x1 * (1.0 + sc2) + sh2
    _to_row_tiles(h2_ref, h2)
    h2_hi = h2.astype(bf16)
    h2_lo = (h2 - h2_hi.astype(f32)).astype(bf16)
    parts = _dot(h2_hi, wr_ref[...]) + _dot(h2_lo, wr_ref[...])
    logits = parts[:, 0:LANES] + parts[:, LANES:2 * LANES]
    route_ref[...] = _route(logits)


def _merge(xc, xd, ya, ys, yg, mod, w_pad, w_br, w_o, ln_g, ln_b, w_r, layer):
    assert OFF_GATE // GATE_BLOCK == 1 and OFF_GATE % GATE_BLOCK == LANE_SHIFT
    const = lambda shape: pl.BlockSpec(shape, lambda i: (0,) * len(shape))
    row = lambda width: pl.BlockSpec((TM, width), lambda i: (i, 0))
    return pl.pallas_call(
        _merge_kernel,
        grid=(T_ALL // TM,),
        in_specs=_x_specs() + [
            row(NA_WIDTH), row(SSM_INNER), row(GMLP_WIDTH), const(mod.shape),
            pl.BlockSpec((None, D_MODEL, GATE_BLOCK), lambda i: (layer, 0, 1), pipeline_mode=pl.Buffered(1)),
            const(w_br.shape), const(w_o.shape), const(ln_g.shape), const(ln_b.shape),
            const(w_r.shape),
        ],
        out_specs=[row(D_MODEL), pl.BlockSpec(_rt_shape(TM), lambda i: (i, 0)), row(LANES)],
        out_shape=[
            jax.ShapeDtypeStruct((T_ALL, D_MODEL), f32),
            jax.ShapeDtypeStruct(_rt_shape(T_ALL), f32),
            jax.ShapeDtypeStruct((T_ALL, LANES), f32),
        ],
        scratch_shapes=[pltpu.VMEM((D_MODEL, N_BRANCHES * D_MODEL), bf16)],
        compiler_params=_cparams(1),
        name="merge",
    )(xc, xd, ya, ys, yg, mod, w_pad, w_br, w_o, ln_g, ln_b, w_r)


def _plan_kernel(route_ref, pos_ref, te_ref, misc_ref, cnt_scr, offs_scr):
    phase, i = pl.program_id(0), pl.program_id(1)
    lane = lax.broadcasted_iota(i32, (PLAN_TM, LANES), 1).astype(f32)
    route = route_ref[...]
    hit0 = lane == route[:, 0:1]
    hit1 = lane == route[:, 1:2]
    onehot = jnp.where(hit0 | hit1, 1.0, 0.0)
    tile_counts = jnp.sum(onehot, axis=0, keepdims=True)

    @pl.when((phase == 0) & (i == 0))
    def _():
        cnt_scr[...] = jnp.zeros(cnt_scr.shape, f32)

    @pl.when(phase == 0)
    def _():
        cnt_scr[...] = cnt_scr[...] + tile_counts

    @pl.when((phase == 1) & (i == 0))
    def _():
        lane1 = lax.broadcasted_iota(i32, (8, LANES), 1)
        counts = jnp.broadcast_to(cnt_scr[...], (8, LANES))
        padded = jnp.where(lane1 < MOE_EXPERTS, jnp.ceil(counts * (1.0 / TME)) * TME, 0.0)
        before = (lax.broadcasted_iota(i32, (LANES, LANES), 0) < lax.broadcasted_iota(i32, (LANES, LANES), 1))
        offs = jnp.dot(padded, before.astype(f32), precision=lax.Precision.HIGHEST, preferred_element_type=f32)
        ends = offs + padded
        offs_scr[...] = offs[0:1]
        cnt_scr[...] = jnp.zeros(cnt_scr.shape, f32)
        tile_start = lax.broadcasted_iota(i32, (TE_ROWS, LANES), 0).astype(f32) * TME
        lane2 = lax.broadcasted_iota(i32, (TE_ROWS, LANES), 1)
        passed = jnp.where((lane2 < MOE_EXPERTS) & (tile_start >= ends[0:1]), 1.0, 0.0)
        te = jnp.minimum(jnp.sum(passed, axis=1, keepdims=True), MOE_EXPERTS - 1.0)
        te_ref[...] = jnp.broadcast_to(te, (TE_ROWS, LANES))
        row = lax.broadcasted_iota(i32, (8, LANES), 0)
        last_tile = jnp.where(padded > 0, ends - TME, -1.0)
        misc_ref[...] = jnp.where(row == 0, ends * (1.0 / TME), jnp.where(row == 1, last_tile, 0.0))

    @pl.when(phase == 1)
    def _():
        base = cnt_scr[...]
        ri = lax.broadcasted_iota(i32, (PLAN_TM, PLAN_TM), 0)
        ci = lax.broadcasted_iota(i32, (PLAN_TM, PLAN_TM), 1)
        earlier = jnp.where(ci < ri, 1.0, 0.0).astype(bf16)
        total = _dot(earlier, onehot.astype(bf16)) + (base + offs_scr[...])
        p0 = jnp.sum(jnp.where(hit0, total, 0.0), axis=1, keepdims=True)
        p1 = jnp.sum(jnp.where(hit1, total, 0.0), axis=1, keepdims=True)
        pos_ref[...] = jnp.where(lane == 0, p0, jnp.where(lane == 1, p1, 0.0))
        cnt_scr[...] = base + tile_counts


def _plan(route):
    pos2, te, misc = pl.pallas_call(
        _plan_kernel,
        grid=(2, T_ALL // PLAN_TM),
        in_specs=[pl.BlockSpec((PLAN_TM, LANES), lambda ph, i: (i, 0))],
        out_specs=[
            pl.BlockSpec((PLAN_TM, LANES), lambda ph, i: (ph * i, 0)),
            pl.BlockSpec((TE_ROWS, LANES), lambda ph, i: (0, 0)),
            pl.BlockSpec((8, LANES), lambda ph, i: (0, 0)),
        ],
        out_shape=[
            jax.ShapeDtypeStruct((T_ALL, LANES), f32),
            jax.ShapeDtypeStruct((TE_ROWS, LANES), f32),
            jax.ShapeDtypeStruct((8, LANES), f32),
        ],
        scratch_shapes=[pltpu.VMEM((1, LANES), f32), pltpu.VMEM((1, LANES), f32)],
        compiler_params=_cparams(2),
        name="plan",
    )(route)
    pos = pos2[:, 0:2].astype(i32).reshape(-1)
    tile_expert = te[:N_ETILES, 0].astype(i32)
    n_active = misc[0, MOE_EXPERTS - 1].astype(i32).reshape(1)
    last_tile = misc[1, :MOE_EXPERTS].astype(i32)
    return pos, tile_expert, n_active, last_tile


def _dispatch_kernel(pos_ref, last_ref, h2_ref, xs_hbm, zbuf, sem):
    i = pl.program_id(0)

    @pl.when(i == 0)
    def _():
        zbuf[...] = jnp.zeros(zbuf.shape, f32)
        for wait in (False, True):
            for e in range(MOE_EXPERTS):
                @pl.when(last_ref[e] >= 0)
                def _():
                    cp = pltpu.make_async_copy(zbuf, _rt_rows(xs_hbm, last_ref[e], TME), sem)
                    cp.wait() if wait else cp.start()

    def issue(r, carry):
        p = (i * DISP_TM + r) * 2
        for k in range(2):
            pltpu.make_async_copy(_rt_rows(h2_ref, r), _rt_rows(xs_hbm, pos_ref[p + k]), sem).start(priority=k)
        return carry

    lax.fori_loop(0, DISP_TM, issue, 0, unroll=8)
    for k in range(2):
        pltpu.make_async_copy(h2_ref, _rt_rows(xs_hbm, 0, DISP_TM), sem).wait()


def _dispatch(pos, last_tile, h2):
    return pl.pallas_call(
        _dispatch_kernel,
        grid_spec=pltpu.PrefetchScalarGridSpec(
            num_scalar_prefetch=2,
            grid=(T_ALL // DISP_TM,),
            in_specs=[pl.BlockSpec(_rt_shape(DISP_TM), lambda i, pos, last: (i, 0))],
            out_specs=pl.BlockSpec(memory_space=pl.ANY),
            scratch_shapes=[pltpu.VMEM(_rt_shape(TME), f32), pltpu.SemaphoreType.DMA(())],
        ),
        out_shape=jax.ShapeDtypeStruct(_rt_shape(P_PAD), f32),
        compiler_params=_dma_cparams(),
        name="dispatch",
    )(pos, last_tile, h2)


def _moe_kernel(te_ref, na_ref, x_ref, w1_ref, w3_ref, w2_ref, o_ref, w1_scr, w3_scr, w2_scr):
    i = pl.program_id(0)
    changed = (i == 0) | (te_ref[i] != te_ref[jnp.maximum(i - 1, 0)])

    @pl.when(changed)
    def _():
        w1_scr[...] = w1_ref[...].astype(bf16)
        w3_scr[...] = w3_ref[...].astype(bf16)
        w2_scr[...] = w2_ref[...].astype(bf16)

    @pl.when(i < na_ref[0])
    def _():
        x = _from_row_tiles(x_ref).astype(bf16)
        hid = _silu(_dot(x, w1_scr[...])) * _dot(x, w3_scr[...])
        _to_row_tiles(o_ref, _dot(hid.astype(bf16), w2_scr[...]))

    @pl.when(i >= na_ref[0])
    def _():
        o_ref[...] = jnp.zeros(o_ref.shape, f32)


def _moe(tile_expert, n_active, x_sorted, w1, w3, w2, layer):
    wspec = lambda shape: pl.BlockSpec((None, None) + shape, lambda i, te, na: (layer, te[i], 0, 0))
    xmap = lambda i, te, na: (jnp.maximum(jnp.minimum(i, na[0] - 1), 0), 0)
    return pl.pallas_call(
        _moe_kernel,
        grid_spec=pltpu.PrefetchScalarGridSpec(
            num_scalar_prefetch=2,
            grid=(N_ETILES,),
            in_specs=[
                pl.BlockSpec(_rt_shape(TME), xmap),
                wspec((D_MODEL, MOE_D_FF)), wspec((D_MODEL, MOE_D_FF)), wspec((MOE_D_FF, D_MODEL)),
            ],
            out_specs=pl.BlockSpec(_rt_shape(TME), lambda i, te, na: (i, 0)),
            scratch_shapes=[pltpu.VMEM((D_MODEL, MOE_D_FF), bf16), pltpu.VMEM((D_MODEL, MOE_D_FF), bf16),
                            pltpu.VMEM((MOE_D_FF, D_MODEL), bf16)],
        ),
        out_shape=jax.ShapeDtypeStruct(_rt_shape(P_PAD), f32),
        compiler_params=_cparams(1),
        name="moe",
    )(tile_expert, n_active, x_sorted, w1, w3, w2)


def _combine_kernel(pos_ref, x1_ref, route_ref, mod_ref, lng_ref, lnb_ref, y_hbm, oc_ref, od_ref, buf, sem):
    i = pl.program_id(0)

    def issue(r, carry):
        p = (i * TM + r) * 2
        for k in range(2):
            pltpu.make_async_copy(_rt_rows(y_hbm, pos_ref[p + k]), _rt_rows(buf.at[k], r), sem).start(priority=k)
        return carry

    lax.fori_loop(0, TM, issue, 0, unroll=8)
    m = mod_ref[pl.ds(_row_type(i, TM), 1), :]
    g2 = m[:, 5 * D_MODEL:6 * D_MODEL]
    route = route_ref[...]
    for s in range(2):
        pltpu.make_async_copy(_rt_rows(y_hbm, 0, TM), buf.at[s], sem).wait()
    ff = _from_row_tiles(buf.at[0]) * route[:, 2:3] + _from_row_tiles(buf.at[1]) * route[:, 3:4]
    out = _layer_norm(DEEPNORM_ALPHA * x1_ref[...] + g2 * ff, lng_ref[...], lnb_ref[...])

    @pl.when(i < N_CTX_TILES)
    def _():
        oc_ref[...] = out

    @pl.when(i >= N_CTX_TILES)
    def _():
        od_ref[...] = out


def _combine(pos, x1, route, mod, ln_g, ln_b, y_sorted):
    const = lambda shape: pl.BlockSpec(shape, lambda i, pos: (0,) * len(shape))
    row = lambda width: pl.BlockSpec((TM, width), lambda i, pos: (i, 0))
    return pl.pallas_call(
        _combine_kernel,
        grid_spec=pltpu.PrefetchScalarGridSpec(
            num_scalar_prefetch=1,
            grid=(T_ALL // TM,),
            in_specs=[row(D_MODEL), row(LANES), const(mod.shape), const(ln_g.shape), const(ln_b.shape),
                      pl.BlockSpec(memory_space=pl.ANY)],
            out_specs=_x_specs(),
            scratch_shapes=[pltpu.VMEM((2,) + _rt_shape(TM), f32), pltpu.SemaphoreType.DMA(())],
        ),
        out_shape=[jax.ShapeDtypeStruct((T_CTX, D_MODEL), f32), jax.ShapeDtypeStruct((T_DN, D_MODEL), f32)],
        compiler_params=_dma_cparams(),
        name="combine",
    )(pos, x1, route, mod, ln_g, ln_b, y_sorted)


def _pad_lanes(row):
    return jnp.pad(row, ((0, 0), (0, LANES - row.shape[1])))


def kernel(x_prompt, x_sample, cache_k, cache_v, state_ssm, c, c_ctx, w_ada, b_ada, w_in, na_rpb, ssm_conv_w, ssm_conv_b, ssm_a_log, ssm_dt_bias, ssm_d, ssm_norm_g, gmlp_ln_g, gmlp_ln_b, gmlp_ws, gmlp_bs, w_branch, w_o, ln_g, ln_b, moe_w_coarse, moe_w_fine, moe_w1, moe_w3, moe_w2):
    cvec = jnp.concatenate([c_ctx[None], c, jnp.zeros((N_ROWTYPES - 1 - DEC_BATCH, D_MODEL), f32)], axis=0)
    mods = _mods(cvec, w_ada, b_ada)
    cos, sin = _rope_tables()
    xc, xd = x_prompt.reshape(T_CTX, D_MODEL), x_sample.reshape(T_DN, D_MODEL)
    w_pad = jnp.pad(w_in, ((0, 0), (0, 0), (0, W_PAD - N_IN)))
    kc = vc = st = None
    for l in range(DEPTH):
        mod = mods[l]
        proj = _inproj(xc, xd, mod, w_pad, l)

        ya, kc, vc = _ctx_attn(proj, l, kc, vc)
        ya = _na_attn(proj, cache_k, cache_v, _na_bias(na_rpb[l]), cos, sin, l, ya)

        ssd_params = (
            ssm_conv_w[l], ssm_conv_b[l][None],
            _pad_lanes(ssm_a_log[l].reshape(1, -1)), _pad_lanes(ssm_dt_bias[l].reshape(1, -1)),
            jnp.repeat(ssm_d[l], SSM_HEAD_DIM)[None], ssm_norm_g[l][None],
        )
        if st is None:
            ys, st = _ssd(proj, ssd_params, l, latent=False)
        else:
            ys, st = _ssd(proj, ssd_params, l, latent=False, st_prev=st)
        (ys,) = _ssd(proj, ssd_params, l, latent=True, h0=state_ssm, y_prev=ys)

        bs_full = jnp.repeat(gmlp_bs[l].T, GMLP_WIDTH // GMLP_GROUPS, axis=1)
        yg = _gmlp(proj, gmlp_ln_g[l][None], gmlp_ln_b[l][None], gmlp_ws[l], bs_full)

        w_r = _pad_lanes(jnp.concatenate([moe_w_coarse[l], moe_w_fine[l]], axis=1))
        w_r_hi = w_r.astype(bf16)
        w_r = jnp.concatenate([w_r_hi, (w_r - w_r_hi.astype(f32)).astype(bf16)], axis=1)
        x1, h2, route = _merge(xc, xd, ya, ys, yg, mod, w_pad, w_branch[l].astype(bf16), w_o[l].astype(bf16),
                               ln_g[l, 0][None], ln_b[l, 0][None], w_r, l)

        pos, tile_expert, n_active, last_tile = _plan(route)
        x_sorted = _dispatch(pos, last_tile, h2)
        y_sorted = _moe(tile_expert, n_active, x_sorted, moe_w1, moe_w3, moe_w2, l)
        xc, xd = _combine(pos, x1, route, mod, ln_g[l, 1][None], ln_b[l, 1][None], y_sorted)

    return (xc.reshape(BATCH, SEQ, D_MODEL), xd.reshape(DEC_BATCH, DEC_SEQ, D_MODEL), kc, vc, st)
```

```python
import functools
import math

import numpy as np
import jax
import jax.numpy as jnp
from jax import lax
from jax.experimental import pallas as pl
from jax.experimental.pallas import tpu as pltpu

f32 = jnp.float32
bf16 = jnp.bfloat16
i32 = jnp.int32

D_MODEL = 1024
BATCH = 32
SEQ = 256
DEPTH = 2
DEC_BATCH = 2
DEC_SEQ = 1024
PAST_LEN = 256
GRID_W = 64
NA_HEADS = 8
NA_HEAD_DIM = 64
NA_WIDTH = NA_HEADS * NA_HEAD_DIM
NA_WIN_ROWS = 8
NA_WIN_COLS = 16
ROPE_BASE = 10000.0
SSM_HEADS = 8
SSM_HEAD_DIM = 64
SSM_INNER = SSM_HEADS * SSM_HEAD_DIM
SSM_GROUPS = 2
SSM_STATE = 64
SSM_CONV_DIM = SSM_INNER + 2 * SSM_GROUPS * SSM_STATE
CHUNK = 128
GMLP_GROUPS = 4
GMLP_WIDTH = 512
N_BRANCHES = 3
MOE_GROUPS = 4
MOE_EPG = 8
MOE_EXPERTS = MOE_GROUPS * MOE_EPG
MOE_D_FF = 256
DEEPNORM_ALPHA = (2 * DEPTH) ** 0.25
LN_EPS = 1e-5
RMS_EPS = 1e-5

OFF_Q = 0
OFF_Z = 3 * NA_WIDTH
OFF_XBC = OFF_Z + SSM_INNER
OFF_DT = OFF_XBC + SSM_CONV_DIM
OFF_U = OFF_DT + 2 * SSM_HEADS
OFF_GATE = OFF_U + 2 * GMLP_WIDTH
N_IN = OFF_GATE + N_BRANCHES * D_MODEL

LANES = 128
P_Q, P_K, P_V, P_Z, P_X, P_U, P_GV = 0, 512, 1024, 1536, 2048, 2560, 3072
P_B, P_C, P_DT = 3584, 3712, 3840
N_PROJ = 3968

T_CTX = BATCH * SEQ
T_DN = DEC_BATCH * DEC_SEQ
T_ALL = T_CTX + T_DN
N_ROWTYPES = 8

TM = 256
TME = 256
N_PAIRS = 2 * T_ALL
N_ETILES = (N_PAIRS + MOE_EXPERTS * (TME - 1) + TME - 1) // TME
P_PAD = N_ETILES * TME
PLAN_TM = 1024
TE_ROWS = 128
DISP_TM = 512

VMEM_LIMIT = 56 * 1024 * 1024


def _cparams(n_axes):
    return pltpu.CompilerParams(dimension_semantics=("arbitrary",) * n_axes, vmem_limit_bytes=VMEM_LIMIT)


def _dma_cparams():
    return pltpu.CompilerParams(dimension_semantics=("arbitrary",), vmem_limit_bytes=VMEM_LIMIT,
                                disable_bounds_checks=True)


def _row_type(tile, rows_per_tile):
    start = tile * rows_per_tile
    return jnp.where(start < T_CTX, 0, 1 + (start - T_CTX) // DEC_SEQ)


def _silu(x):
    return x * jax.nn.sigmoid(x)


def _gelu(x):
    return 0.5 * x * (1.0 + lax.erf(x * (1.0 / math.sqrt(2.0))))


def _dot(a, b):
    return jnp.dot(a, b, preferred_element_type=f32)


def _dot_nt(a, b):
    return lax.dot_general(a, b, (((1,), (1,)), ((), ())), preferred_element_type=f32)


def _dot_tn(a, b):
    return lax.dot_general(a, b, (((0,), (0,)), ((), ())), preferred_element_type=f32)


RT = D_MODEL // LANES


def _rt_shape(n):
    return (n * RT, LANES)


def _rt_rows(ref, start, n=1):
    return ref.at[pl.ds(pl.multiple_of(start * RT, RT), n * RT)]


def _to_row_tiles(ref, value):
    n = value.shape[0]
    for s in range(RT):
        ref[pl.ds(s, n, stride=RT), :] = value[:, s * LANES:(s + 1) * LANES]


def _from_row_tiles(ref):
    n = ref.shape[0] // RT
    return jnp.concatenate([ref[pl.ds(s, n, stride=RT), :] for s in range(RT)], axis=1)


W_PAD = 2 * 3840
GATE_BLOCK = 3840
TAIL_BLOCK = 1408
LANE_SHIFT = OFF_U % LANES
N_CTX_TILES = T_CTX // TM


def _unshift_columns(w, n_out):
    lane = lax.broadcasted_iota(i32, (w.shape[0], LANES), 1)
    rolled = [pltpu.roll(w[:, b * LANES:(b + 1) * LANES], LANES - LANE_SHIFT, 1) for b in range(n_out + 1)]
    return [jnp.where(lane < LANES - LANE_SHIFT, rolled[b], rolled[b + 1]).astype(bf16) for b in range(n_out)]


def _x_specs(imap=lambda i: i):
    ctx = pl.BlockSpec((TM, D_MODEL), lambda i, *_: (jnp.minimum(imap(i), N_CTX_TILES - 1), 0))
    lat = pl.BlockSpec((TM, D_MODEL), lambda i, *_: (jnp.maximum(imap(i) - N_CTX_TILES, 0), 0))
    return [ctx, lat]


def _load_x(xc_ref, xd_ref):
    return jnp.where(pl.program_id(0) < N_CTX_TILES, xc_ref[...], xd_ref[...])


def _mods_kernel(c_ref, w_ref, b_ref, o_ref):
    o_ref[...] = _dot(_silu(c_ref[...]).astype(bf16), w_ref[...].astype(bf16)) + b_ref[...]


def _mods(cvec, w_ada, b_ada):
    tn = 1536
    return pl.pallas_call(
        _mods_kernel,
        grid=(DEPTH, 6 * D_MODEL // tn),
        in_specs=[
            pl.BlockSpec((N_ROWTYPES, D_MODEL), lambda l, j: (0, 0)),
            pl.BlockSpec((None, D_MODEL, tn), lambda l, j: (l, 0, j)),
            pl.BlockSpec((None, 1, tn), lambda l, j: (l, 0, j)),
        ],
        out_specs=pl.BlockSpec((None, N_ROWTYPES, tn), lambda l, j: (l, 0, j)),
        out_shape=jax.ShapeDtypeStruct((DEPTH, N_ROWTYPES, 6 * D_MODEL), f32),
        compiler_params=_cparams(2),
        name="mods",
    )(cvec, w_ada, b_ada.reshape(DEPTH, 1, 6 * D_MODEL))


def _inproj_kernel(xc_ref, xd_ref, mod_ref, wh_ref, wt_ref, o_ref, wh_scr, wt_scr):
    @pl.when(pl.program_id(0) == 0)
    def _():
        wh_scr[...] = wh_ref[...].astype(bf16)
        wt = wt_ref[...]
        lane = lax.broadcasted_iota(i32, (D_MODEL, LANES), 1)
        wt_scr[:, 0:LANES] = jnp.where(lane < LANE_SHIFT, wt[:, 0:LANES], 0.0).astype(bf16)
        for b, blk in enumerate(_unshift_columns(wt, 2 * GMLP_WIDTH // LANES)):
            wt_scr[:, (b + 1) * LANES:(b + 2) * LANES] = blk

    m = mod_ref[pl.ds(_row_type(pl.program_id(0), TM), 1), :]
    h = (_load_x(xc_ref, xd_ref) * (1.0 + m[:, D_MODEL:2 * D_MODEL]) + m[:, 0:D_MODEL]).astype(bf16)
    head = _dot(h, wh_scr[...])
    tail = _dot(h, wt_scr[...])
    n_bc = 2 * SSM_GROUPS * SSM_STATE
    o_ref[:, 0:P_U] = head[:, 0:P_U]
    o_ref[:, P_U:P_B] = tail[:, LANES:]
    o_ref[:, P_B:P_B + n_bc] = head[:, P_U:P_U + n_bc]
    o_ref[:, P_DT:N_PROJ] = tail[:, 0:LANES]


def _inproj(xc, xd, mod, w_pad, layer):
    assert OFF_DT % TAIL_BLOCK == 0 and TAIL_BLOCK >= OFF_GATE - OFF_DT + LANES
    return pl.pallas_call(
        _inproj_kernel,
        grid=(T_ALL // TM,),
        in_specs=_x_specs() + [
            pl.BlockSpec((N_ROWTYPES, 6 * D_MODEL), lambda i: (0, 0)),
            pl.BlockSpec((None, D_MODEL, OFF_DT), lambda i: (layer, 0, 0), pipeline_mode=pl.Buffered(1)),
            pl.BlockSpec((None, D_MODEL, TAIL_BLOCK), lambda i: (layer, 0, OFF_DT // TAIL_BLOCK),
                         pipeline_mode=pl.Buffered(1)),
        ],
        out_specs=pl.BlockSpec((TM, N_PROJ), lambda i: (i, 0)),
        out_shape=jax.ShapeDtypeStruct((T_ALL, N_PROJ), f32),
        scratch_shapes=[pltpu.VMEM((D_MODEL, OFF_DT), bf16), pltpu.VMEM((D_MODEL, LANES + 2 * GMLP_WIDTH), bf16)],
        compiler_params=_cparams(1),
        name="inproj",
    )(xc, xd, mod, w_pad, w_pad)


def _ctx_attn_kernel(q_ref, k_ref, v_ref, y_ref, kc_ref, vc_ref):
    scale = NA_HEAD_DIM ** -0.5
    outs = []
    for h in range(NA_HEADS):
        sl = slice(h * NA_HEAD_DIM, (h + 1) * NA_HEAD_DIM)
        q, k, v = q_ref[:, sl], k_ref[:, sl], v_ref[:, sl]
        kc_ref[h] = k
        vc_ref[h] = v
        s = _dot_nt(q.astype(bf16), k.astype(bf16)) * scale
        e = jnp.exp(s - jnp.max(s, -1, keepdims=True))
        p = e / jnp.sum(e, -1, keepdims=True)
        outs.append(_dot(p.astype(bf16), v.astype(bf16)))
    y_ref[...] = jnp.concatenate(outs, axis=1)


def _ctx_attn(proj, layer, kc_prev, vc_prev):
    cache_shape = jax.ShapeDtypeStruct((BATCH, DEPTH, NA_HEADS, SEQ, NA_HEAD_DIM), f32)
    in_specs = [
        pl.BlockSpec((SEQ, NA_WIDTH), lambda b: (b, P_Q // NA_WIDTH)),
        pl.BlockSpec((SEQ, NA_WIDTH), lambda b: (b, P_K // NA_WIDTH)),
        pl.BlockSpec((SEQ, NA_WIDTH), lambda b: (b, P_V // NA_WIDTH)),
    ]
    args = [proj, proj, proj]
    aliases = {}
    kernel = _ctx_attn_kernel
    if kc_prev is not None:
        in_specs += [pl.BlockSpec(memory_space=pl.ANY)] * 2
        args += [kc_prev, vc_prev]
        aliases = {3: 1, 4: 2}
        kernel = lambda q, k, v, _kc, _vc, y, kc, vc: _ctx_attn_kernel(q, k, v, y, kc, vc)
    cache_spec = pl.BlockSpec((None, None, NA_HEADS, SEQ, NA_HEAD_DIM), lambda b: (b, layer, 0, 0, 0))
    return pl.pallas_call(
        kernel,
        grid=(BATCH,),
        in_specs=in_specs,
        out_specs=[pl.BlockSpec((SEQ, NA_WIDTH), lambda b: (b, 0)), cache_spec, cache_spec],
        out_shape=[jax.ShapeDtypeStruct((T_ALL, NA_WIDTH), f32), cache_shape, cache_shape],
        input_output_aliases=aliases,
        compiler_params=_cparams(1),
        name="ctx_attn",
    )(*args)


GRID_ROWS = DEC_SEQ // GRID_W
N_WIN = NA_WIN_ROWS * GRID_W


def _rope_tables():
    quarter = NA_HEAD_DIM // 4
    freqs = ROPE_BASE ** (-np.arange(quarter, dtype=np.float32) / quarter)
    lane = np.arange(LANES)
    j = lane % NA_HEAD_DIM
    use_col = (j // (NA_HEAD_DIM // 2)) == 1
    fi = j % quarter
    second = (j % (NA_HEAD_DIM // 2)) >= quarter
    pos = jnp.arange(DEC_SEQ)
    p = jnp.where(use_col[None, :], (pos % GRID_W)[:, None], (pos // GRID_W)[:, None]).astype(f32)
    ang = p * jnp.asarray(freqs[fi], f32)[None, :]
    cos = jnp.cos(ang)
    sin = jnp.sin(ang)
    return cos, jnp.where(second[None, :], sin, -sin)


def _na_bias(rpb):
    qc = np.arange(GRID_W)[:, None]
    kc = np.arange(GRID_W)[None, :]
    col0 = np.clip(qc - NA_WIN_COLS // 2, 0, GRID_W - NA_WIN_COLS)
    valid = (kc >= col0) & (kc < col0 + NA_WIN_COLS)
    dc = np.clip(kc - qc + NA_WIN_COLS - 1, 0, 2 * NA_WIN_COLS - 2)
    onehot = (dc[None] == np.arange(2 * NA_WIN_COLS - 1)[:, None, None]).astype(np.float32)
    picked = jnp.einsum('hdr,rqk->hdqk', rpb.astype(f32), jnp.asarray(onehot), precision=lax.Precision.HIGHEST)
    tb = jnp.where(valid[None, None], picked, -jnp.inf)
    rows = [jnp.concatenate([tb[:, d0 + i] for i in range(NA_WIN_ROWS)], axis=-1) for d0 in range(NA_WIN_ROWS)]
    return jnp.stack(rows, axis=1)


def _rope(x, cos, sin_signed):
    lane = lax.broadcasted_iota(i32, x.shape, 1)
    first = (lane % (NA_HEAD_DIM // 2)) < (NA_HEAD_DIM // 4)
    q = NA_HEAD_DIM // 4
    partner = jnp.where(first, pltpu.roll(x, LANES - q, 1), pltpu.roll(x, q, 1))
    return x * cos + partner * sin_signed


def _na_attn_kernel(q_ref, k_ref, v_ref, kctx_ref, vctx_ref, bias_ref, cos_ref, sin_ref, _y_in, y_ref,
                    q_scr, k_scr, v_scr, sc_scr, y_scr):
    scale = NA_HEAD_DIM ** -0.5
    cos, sin = cos_ref[...], sin_ref[...]
    q = _rope(q_ref[...], cos, sin).astype(bf16)
    k = _rope(k_ref[...], cos, sin).astype(bf16)
    v = v_ref[...].astype(bf16)
    for hh in range(2):
        sl = slice(hh * NA_HEAD_DIM, (hh + 1) * NA_HEAD_DIM)
        q_scr[hh] = q[:, sl]
        k_scr[hh] = k[:, sl]
        v_scr[hh] = v[:, sl]
        sc_scr[hh] = _dot_nt(q[:, sl], kctx_ref[hh].astype(bf16)) * scale
    for hh in range(2):
        vctx = vctx_ref[hh].astype(bf16)

        def row_body(r, carry):
            kr0 = jnp.clip(r - NA_WIN_ROWS // 2, 0, GRID_ROWS - NA_WIN_ROWS)
            d0 = kr0 - r + NA_WIN_ROWS - 1
            q0 = pl.multiple_of(r * GRID_W, GRID_W)
            k0 = pl.multiple_of(kr0 * GRID_W, GRID_W)
            qr = q_scr[hh, pl.ds(q0, GRID_W), :]
            kw = k_scr[hh, pl.ds(k0, N_WIN), :]
            vw = v_scr[hh, pl.ds(k0, N_WIN), :]
            s_win = _dot_nt(qr, kw) * scale + bias_ref[hh, d0]
            s_ctx = sc_scr[hh, pl.ds(q0, GRID_W), :]
            m = jnp.maximum(jnp.max(s_win, -1, keepdims=True), jnp.max(s_ctx, -1, keepdims=True))
            e_win = jnp.exp(s_win - m)
            e_ctx = jnp.exp(s_ctx - m)
            denom = jnp.sum(e_win, -1, keepdims=True) + jnp.sum(e_ctx, -1, keepdims=True)
            o = _dot(e_win.astype(bf16), vw) + _dot(e_ctx.astype(bf16), vctx)
            y_scr[hh, pl.ds(q0, GRID_W), :] = o / denom
            return carry

        lax.fori_loop(0, GRID_ROWS, row_body, 0)
    y_ref[...] = jnp.concatenate([y_scr[0], y_scr[1]], axis=1)


def _na_attn(proj, cache_k, cache_v, bias, cos, sin, layer, y_att):
    dn0 = T_CTX // DEC_SEQ
    blk = lambda off: pl.BlockSpec((DEC_SEQ, LANES), lambda b, hp: (dn0 + b, off // LANES + hp))
    ctx_spec = pl.BlockSpec((None, None, 2, PAST_LEN, NA_HEAD_DIM), lambda b, hp: (b, layer, hp, 0, 0))
    return pl.pallas_call(
        _na_attn_kernel,
        grid=(DEC_BATCH, NA_HEADS // 2),
        in_specs=[
            blk(P_Q), blk(P_K), blk(P_V), ctx_spec, ctx_spec,
            pl.BlockSpec((2, NA_WIN_ROWS, GRID_W, N_WIN), lambda b, hp: (hp, 0, 0, 0)),
            pl.BlockSpec((DEC_SEQ, LANES), lambda b, hp: (0, 0)),
            pl.BlockSpec((DEC_SEQ, LANES), lambda b, hp: (0, 0)),
            pl.BlockSpec(memory_space=pl.ANY),
        ],
        out_specs=pl.BlockSpec((DEC_SEQ, LANES), lambda b, hp: (dn0 + b, hp)),
        out_shape=jax.ShapeDtypeStruct((T_ALL, NA_WIDTH), f32),
        scratch_shapes=[
            pltpu.VMEM((2, DEC_SEQ, NA_HEAD_DIM), bf16),
            pltpu.VMEM((2, DEC_SEQ, NA_HEAD_DIM), bf16),
            pltpu.VMEM((2, DEC_SEQ, NA_HEAD_DIM), bf16),
            pltpu.VMEM((2, DEC_SEQ, PAST_LEN), f32),
            pltpu.VMEM((2, DEC_SEQ, NA_HEAD_DIM), f32),
        ],
        input_output_aliases={8: 0},
        compiler_params=_cparams(2),
        name="na_attn",
    )(proj, proj, proj, cache_k, cache_v, bias, cos, sin, y_att)


def _conv_silu(v, w, b):
    n = v.shape[0]
    rows = lax.broadcasted_iota(i32, v.shape, 0)
    prev = jnp.where(rows == 0, 0.0, pltpu.roll(v, 1, 0))
    nxt = jnp.where(rows == n - 1, 0.0, pltpu.roll(v, n - 1, 0))
    return _silu(prev * w[0:1] + v * w[1:2] + nxt * w[2:3] + b)


def _ssd_kernel(*refs, seq_len, has_h0, emit_state, aliased):
    refs = list(refs)
    z_ref, x_ref, b_ref, c_ref, dt_ref, cw_ref, cb_ref, alog_ref, dtb_ref, dskip_ref, ng_ref = refs[:11]
    pos = 11
    h0_ref = None
    if has_h0:
        h0_ref = refs[pos]
        pos += 1
    pos += aliased
    y_ref = refs[pos]
    pos += 1
    st_ref = None
    if emit_state:
        st_ref = refs[pos]
        pos += 1
    xs_scr, b_scr, c_scr, dt_scr, dta_scr, y_scr, h_scr = refs[pos:]

    n_chunks = seq_len // CHUNK
    cw = cw_ref[...]
    cb = cb_ref[...]
    xs = _conv_silu(x_ref[...], cw[:, 0:SSM_INNER], cb[:, 0:SSM_INNER])
    xs_scr[...] = xs
    gn = SSM_GROUPS * SSM_STATE
    b_scr[...] = _conv_silu(b_ref[...], cw[:, SSM_INNER:SSM_INNER + gn], cb[:, SSM_INNER:SSM_INNER + gn])
    c_scr[...] = _conv_silu(c_ref[...], cw[:, SSM_INNER + gn:], cb[:, SSM_INNER + gn:])
    raw = dt_ref[...] + dtb_ref[...]
    dt = jnp.maximum(raw, 0.0) + jnp.log1p(jnp.exp(-jnp.abs(raw)))
    dt_scr[...] = dt
    dta_scr[...] = dt * (-jnp.exp(alog_ref[...]))
    y_scr[...] = xs * dskip_ref[...]
    rep = SSM_HEADS // SSM_GROUPS
    head_rows = lambda hh: slice(hh * SSM_HEAD_DIM, (hh + 1) * SSM_HEAD_DIM)
    if has_h0:
        for d in range(2):
            for h in range(SSM_HEADS):
                h_scr[d, h // rep, head_rows(h % rep), :] = h0_ref[d, h]
    else:
        h_scr[...] = jnp.zeros(h_scr.shape, f32)

    ri = lax.broadcasted_iota(i32, (CHUNK, CHUNK), 0)
    ci = lax.broadcasted_iota(i32, (CHUNK, CHUNK), 1)
    keep = (ri >= ci, ri <= ci)
    def one_chunk(direction, chunk):
        r0 = pl.multiple_of(chunk * CHUNK, CHUNK)
        rows = pl.ds(r0, CHUNK)
        mask = keep[direction]
        cum = jnp.dot(mask.astype(f32), dta_scr[rows, :], precision=lax.Precision.HIGHEST,
                      preferred_element_type=f32)
        cum_t = cum.T
        end_row = CHUNK - 1 if direction == 0 else 0
        chunk_decay = jnp.exp(cum[end_row:end_row + 1, :])
        bc = b_scr[rows, :].astype(bf16)
        cc = c_scr[rows, :].astype(bf16)
        xc = xs_scr[rows, :]
        dtc = dt_scr[rows, :]
        gw = rep * SSM_HEAD_DIM
        head_of_lane = lax.broadcasted_iota(i32, (CHUNK, gw), 1) // SSM_HEAD_DIM
        ys = []
        for g in range(SSM_GROUPS):
            bg = bc[:, g * SSM_STATE:(g + 1) * SSM_STATE]
            cg = cc[:, g * SSM_STATE:(g + 1) * SSM_STATE]
            cb_g = _dot_nt(cg, bg)
            scores, cum_x, dt_x, decay_rows = [], None, None, []
            for hh in range(rep):
                ln = direction * SSM_HEADS + g * rep + hh
                col = jnp.broadcast_to(cum[:, ln:ln + 1], (CHUNK, CHUNK))
                decay = jnp.exp(jnp.where(mask, col - cum_t[ln:ln + 1, :], -jnp.inf))
                scores.append((cb_g * decay).astype(bf16))
                col_w = jnp.concatenate([col] * (gw // CHUNK), axis=1)
                dt_w = jnp.broadcast_to(dtc[:, ln:ln + 1], (CHUNK, gw))
                cum_x = col_w if hh == 0 else jnp.where(head_of_lane == hh, col_w, cum_x)
                dt_x = dt_w if hh == 0 else jnp.where(head_of_lane == hh, dt_w, dt_x)
                decay_rows.append(jnp.broadcast_to(chunk_decay[:, ln:ln + 1], (SSM_HEAD_DIM, SSM_STATE)))
            xdt = xc[:, g * gw:(g + 1) * gw] * dt_x
            to_end = jnp.exp(cum_x[end_row:end_row + 1, :] - cum_x)
            rhs = jnp.concatenate([jnp.where(head_of_lane == hh, xdt, 0.0).astype(bf16) for hh in range(rep)], axis=0)
            h_prev = h_scr[direction, g]
            y = _dot(jnp.concatenate(scores, axis=1), rhs)
            y = y + _dot_nt(cg, h_prev.astype(bf16)) * jnp.exp(cum_x)
            states = _dot_tn((xdt * to_end).astype(bf16), bg)
            h_scr[direction, g] = h_prev * jnp.concatenate(decay_rows, axis=0) + states
            ys.append(y)
        y_scr[rows, :] = y_scr[rows, :] + jnp.concatenate(ys, axis=1)

    def step(s, carry):
        one_chunk(0, s)
        one_chunk(1, n_chunks - 1 - s)
        return carry

    lax.fori_loop(0, n_chunks, step, 0)

    z = z_ref[...]
    y = y_scr[...] * _silu(z)
    y = y * lax.rsqrt(jnp.mean(jnp.square(y), -1, keepdims=True) + RMS_EPS)
    y_ref[...] = y * ng_ref[...]
    if emit_state:
        for d in range(2):
            for h in range(SSM_HEADS):
                st_ref[d, h] = h_scr[d, h // rep, head_rows(h % rep), :]


def _ssd(proj, params, layer, *, latent, h0=None, y_prev=None, st_prev=None):
    seq_len = DEC_SEQ if latent else SEQ
    n_seq = DEC_BATCH if latent else BATCH
    blk0 = (T_CTX // DEC_SEQ) if latent else 0
    blk = lambda width, off: pl.BlockSpec((seq_len, width), lambda b: (blk0 + b, off // width))
    const = lambda shape: pl.BlockSpec(shape, lambda b: (0,) * len(shape))
    conv_w, conv_b, a_log_row, dt_bias_row, d_skip_row, norm_g = params
    in_specs = [
        blk(SSM_INNER, P_Z), blk(SSM_INNER, P_X), blk(LANES, P_B), blk(LANES, P_C), blk(LANES, P_DT),
        const(conv_w.shape), const(conv_b.shape), const(a_log_row.shape), const(dt_bias_row.shape),
        const(d_skip_row.shape), const(norm_g.shape),
    ]
    args = [proj, proj, proj, proj, proj, conv_w, conv_b, a_log_row, dt_bias_row, d_skip_row, norm_g]
    state_block = (None, None, 2, SSM_HEADS, SSM_HEAD_DIM, SSM_STATE)
    if h0 is not None:
        in_specs.append(pl.BlockSpec(state_block, lambda b: (b, layer, 0, 0, 0, 0)))
        args.append(h0)
    aliases = {}
    n_alias = 0
    out_specs = [pl.BlockSpec((seq_len, SSM_INNER), lambda b: (blk0 + b, 0))]
    out_shape = [jax.ShapeDtypeStruct((T_ALL, SSM_INNER), f32)]
    if y_prev is not None:
        aliases[len(args)] = 0
        in_specs.append(pl.BlockSpec(memory_space=pl.ANY))
        args.append(y_prev)
        n_alias += 1
    emit_state = not latent
    if emit_state:
        out_specs.append(pl.BlockSpec(state_block, lambda b: (b, layer, 0, 0, 0, 0)))
        out_shape.append(jax.ShapeDtypeStruct((BATCH, DEPTH, 2, SSM_HEADS, SSM_HEAD_DIM, SSM_STATE), f32))
        if st_prev is not None:
            aliases[len(args)] = 1
            in_specs.append(pl.BlockSpec(memory_space=pl.ANY))
            args.append(st_prev)
            n_alias += 1
    return pl.pallas_call(
        functools.partial(_ssd_kernel, seq_len=seq_len, has_h0=h0 is not None, emit_state=emit_state,
                          aliased=n_alias),
        grid=(n_seq,),
        in_specs=in_specs,
        out_specs=out_specs,
        out_shape=out_shape,
        scratch_shapes=[
            pltpu.VMEM((seq_len, SSM_INNER), f32),
            pltpu.VMEM((seq_len, LANES), f32),
            pltpu.VMEM((seq_len, LANES), f32),
            pltpu.VMEM((seq_len, LANES), f32),
            pltpu.VMEM((seq_len, LANES), f32),
            pltpu.VMEM((seq_len, SSM_INNER), f32),
            pltpu.VMEM((2, SSM_GROUPS, SSM_HEADS // SSM_GROUPS * SSM_HEAD_DIM, SSM_STATE), f32),
        ],
        input_output_aliases=aliases,
        compiler_params=_cparams(1),
        name="ssd_latent" if latent else "ssd_ctx",
    )(*args)


def _gmlp_kernel(u_ref, v_ref, g_ref, b_ref, ws_ref, bs_ref, o_ref):
    u = _gelu(u_ref[...])
    v = _gelu(v_ref[...])
    mu = jnp.mean(v, -1, keepdims=True)
    var = jnp.mean(jnp.square(v - mu), -1, keepdims=True)
    v = ((v - mu) * lax.rsqrt(var + LN_EPS) * g_ref[...] + b_ref[...]).astype(bf16)
    gd = GMLP_WIDTH // GMLP_GROUPS
    rows = []
    for c in range(TM // CHUNK):
        cols = [_dot(ws_ref[g].astype(bf16), v[c * CHUNK:(c + 1) * CHUNK, g * gd:(g + 1) * gd])
                for g in range(GMLP_GROUPS)]
        rows.append(jnp.concatenate(cols, axis=1) + bs_ref[...])
    o_ref[...] = u * jnp.concatenate(rows, axis=0)


def _gmlp(proj, ln_g, ln_b, ws, bs_full):
    const = lambda shape: pl.BlockSpec(shape, lambda i: (0,) * len(shape))
    return pl.pallas_call(
        _gmlp_kernel,
        grid=(T_ALL // TM,),
        in_specs=[
            pl.BlockSpec((TM, GMLP_WIDTH), lambda i: (i, P_U // GMLP_WIDTH)),
            pl.BlockSpec((TM, GMLP_WIDTH), lambda i: (i, P_GV // GMLP_WIDTH)),
            const(ln_g.shape), const(ln_b.shape), const(ws.shape), const(bs_full.shape),
        ],
        out_specs=pl.BlockSpec((TM, GMLP_WIDTH), lambda i: (i, 0)),
        out_shape=jax.ShapeDtypeStruct((T_ALL, GMLP_WIDTH), f32),
        compiler_params=_cparams(1),
        name="gmlp",
    )(proj, proj, ln_g, ln_b, ws, bs_full)


def _layer_norm(x, g, b):
    mu = jnp.mean(x, -1, keepdims=True)
    var = jnp.mean(jnp.square(x - mu), -1, keepdims=True)
    return (x - mu) * lax.rsqrt(var + LN_EPS) * g + b


def _route(logits):
    lane = lax.broadcasted_iota(i32, logits.shape, 1).astype(f32)
    big = 1e9
    lc = jnp.where(lane < MOE_GROUPS, logits, -jnp.inf)
    mc = jnp.max(lc, -1, keepdims=True)
    p_g = 1.0 / jnp.sum(jnp.exp(lc - mc), -1, keepdims=True)
    g_sel = jnp.min(jnp.where(lc == mc, lane, big), -1, keepdims=True)
    lo = MOE_GROUPS + g_sel * MOE_EPG
    fm = jnp.where((lane >= lo) & (lane < lo + MOE_EPG), logits, -jnp.inf)
    v1 = jnp.max(fm, -1, keepdims=True)
    i1 = jnp.min(jnp.where(fm == v1, lane, big), -1, keepdims=True)
    fm2 = jnp.where(lane == i1, -jnp.inf, fm)
    v2 = jnp.max(fm2, -1, keepdims=True)
    i2 = jnp.min(jnp.where(fm2 == v2, lane, big), -1, keepdims=True)
    e2 = jnp.exp(v2 - v1)
    w1 = p_g / (1.0 + e2)
    w2 = p_g * e2 / (1.0 + e2)
    out = jnp.where(lane == 0, i1 - MOE_GROUPS, 0.0)
    out = jnp.where(lane == 1, i2 - MOE_GROUPS, out)
    out = jnp.where(lane == 2, w1, out)
    return jnp.where(lane == 3, w2, out)


def _merge_kernel(xc_ref, xd_ref, ya_ref, ys_ref, yg_ref, mod_ref, wg_ref, wb_ref, wo_ref, lng_ref, lnb_ref, wr_ref,
                  x1_ref, h2_ref, route_ref, wg_scr):
    @pl.when(pl.program_id(0) == 0)
    def _():
        for b, blk in enumerate(_unshift_columns(wg_ref[...], N_BRANCHES * D_MODEL // LANES)):
            wg_scr[:, b * LANES:(b + 1) * LANES] = blk

    m = mod_ref[pl.ds(_row_type(pl.program_id(0), TM), 1), :]
    sh1, sc1, g1 = m[:, 0:D_MODEL], m[:, D_MODEL:2 * D_MODEL], m[:, 2 * D_MODEL:3 * D_MODEL]
    sh2, sc2 = m[:, 3 * D_MODEL:4 * D_MODEL], m[:, 4 * D_MODEL:5 * D_MODEL]
    x = _load_x(xc_ref, xd_ref)
    h = (x * (1.0 + sc1) + sh1).astype(bf16)
    merged = None
    for i, y_ref in enumerate((ya_ref, ys_ref, yg_ref)):
        gate = jax.nn.sigmoid(_dot(h, wg_scr[:, i * D_MODEL:(i + 1) * D_MODEL]))
        term = gate * _dot(y_ref[...].astype(bf16), wb_ref[i])
        merged = term if merged is None else merged + term
    out = _dot(merged.astype(bf16), wo_ref[...])
    x1 = _layer_norm(DEEPNORM_ALPHA * x + g1 * out, lng_ref[...], lnb_ref[...])
    x1_ref[...] = x1
    h2 = x1 * (1.0 + sc2) + sh2
    _to_row_tiles(h2_ref, h2)
    h2_hi = h2.astype(bf16)
    h2_lo = (h2 - h2_hi.astype(f32)).astype(bf16)
    parts = _dot(h2_hi, wr_ref[...]) + _dot(h2_lo, wr_ref[...])
    logits = parts[:, 0:LANES] + parts[:, LANES:2 * LANES]
    route_ref[...] = _route(logits)


def _merge(xc, xd, ya, ys, yg, mod, w_pad, w_br, w_o, ln_g, ln_b, w_r, layer):
    assert OFF_GATE // GATE_BLOCK == 1 and OFF_GATE % GATE_BLOCK == LANE_SHIFT
    const = lambda shape: pl.BlockSpec(shape, lambda i: (0,) * len(shape))
    row = lambda width: pl.BlockSpec((TM, width), lambda i: (i, 0))
    return pl.pallas_call(
        _merge_kernel,
        grid=(T_ALL // TM,),
        in_specs=_x_specs() + [
            row(NA_WIDTH), row(SSM_INNER), row(GMLP_WIDTH), const(mod.shape),
            pl.BlockSpec((None, D_MODEL, GATE_BLOCK), lambda i: (layer, 0, 1), pipeline_mode=pl.Buffered(1)),
            const(w_br.shape), const(w_o.shape), const(ln_g.shape), const(ln_b.shape),
            const(w_r.shape),
        ],
        out_specs=[row(D_MODEL), pl.BlockSpec(_rt_shape(TM), lambda i: (i, 0)), row(LANES)],
        out_shape=[
            jax.ShapeDtypeStruct((T_ALL, D_MODEL), f32),
            jax.ShapeDtypeStruct(_rt_shape(T_ALL), f32),
            jax.ShapeDtypeStruct((T_ALL, LANES), f32),
        ],
        scratch_shapes=[pltpu.VMEM((D_MODEL, N_BRANCHES * D_MODEL), bf16)],
        compiler_params=_cparams(1),
        name="merge",
    )(xc, xd, ya, ys, yg, mod, w_pad, w_br, w_o, ln_g, ln_b, w_r)


def _plan_kernel(route_ref, pos_ref, te_ref, misc_ref, cnt_scr, offs_scr):
    phase, i = pl.program_id(0), pl.program_id(1)
    lane = lax.broadcasted_iota(i32, (PLAN_TM, LANES), 1).astype(f32)
    route = route_ref[...]
    hit0 = lane == route[:, 0:1]
    hit1 = lane == route[:, 1:2]
    onehot = jnp.where(hit0 | hit1, 1.0, 0.0)
    tile_counts = jnp.sum(onehot, axis=0, keepdims=True)

    @pl.when((phase == 0) & (i == 0))
    def _():
        cnt_scr[...] = jnp.zeros(cnt_scr.shape, f32)

    @pl.when(phase == 0)
    def _():
        cnt_scr[...] = cnt_scr[...] + tile_counts

    @pl.when((phase == 1) & (i == 0))
    def _():
        lane1 = lax.broadcasted_iota(i32, (8, LANES), 1)
        counts = jnp.broadcast_to(cnt_scr[...], (8, LANES))
        padded = jnp.where(lane1 < MOE_EXPERTS, jnp.ceil(counts * (1.0 / TME)) * TME, 0.0)
        before = (lax.broadcasted_iota(i32, (LANES, LANES), 0) < lax.broadcasted_iota(i32, (LANES, LANES), 1))
        offs = jnp.dot(padded, before.astype(f32), precision=lax.Precision.HIGHEST, preferred_element_type=f32)
        ends = offs + padded
        offs_scr[...] = offs[0:1]
        cnt_scr[...] = jnp.zeros(cnt_scr.shape, f32)
        tile_start = lax.broadcasted_iota(i32, (TE_ROWS, LANES), 0).astype(f32) * TME
        lane2 = lax.broadcasted_iota(i32, (TE_ROWS, LANES), 1)
        passed = jnp.where((lane2 < MOE_EXPERTS) & (tile_start >= ends[0:1]), 1.0, 0.0)
        te = jnp.minimum(jnp.sum(passed, axis=1, keepdims=True), MOE_EXPERTS - 1.0)
        te_ref[...] = jnp.broadcast_to(te, (TE_ROWS, LANES))
        row = lax.broadcasted_iota(i32, (8, LANES), 0)
        last_tile = jnp.where(padded > 0, ends - TME, -1.0)
        misc_ref[...] = jnp.where(row == 0, ends * (1.0 / TME), jnp.where(row == 1, last_tile, 0.0))

    @pl.when(phase == 1)
    def _():
        base = cnt_scr[...]
        ri = lax.broadcasted_iota(i32, (PLAN_TM, PLAN_TM), 0)
        ci = lax.broadcasted_iota(i32, (PLAN_TM, PLAN_TM), 1)
        earlier = jnp.where(ci < ri, 1.0, 0.0).astype(bf16)
        total = _dot(earlier, onehot.astype(bf16)) + (base + offs_scr[...])
        p0 = jnp.sum(jnp.where(hit0, total, 0.0), axis=1, keepdims=True)
        p1 = jnp.sum(jnp.where(hit1, total, 0.0), axis=1, keepdims=True)
        pos_ref[...] = jnp.where(lane == 0, p0, jnp.where(lane == 1, p1, 0.0))
        cnt_scr[...] = base + tile_counts


def _plan(route):
    pos2, te, misc = pl.pallas_call(
        _plan_kernel,
        grid=(2, T_ALL // PLAN_TM),
        in_specs=[pl.BlockSpec((PLAN_TM, LANES), lambda ph, i: (i, 0))],
        out_specs=[
            pl.BlockSpec((PLAN_TM, LANES), lambda ph, i: (ph * i, 0)),
            pl.BlockSpec((TE_ROWS, LANES), lambda ph, i: (0, 0)),
            pl.BlockSpec((8, LANES), lambda ph, i: (0, 0)),
        ],
        out_shape=[
            jax.ShapeDtypeStruct((T_ALL, LANES), f32),
            jax.ShapeDtypeStruct((TE_ROWS, LANES), f32),
            jax.ShapeDtypeStruct((8, LANES), f32),
        ],
        scratch_shapes=[pltpu.VMEM((1, LANES), f32), pltpu.VMEM((1, LANES), f32)],
        compiler_params=_cparams(2),
        name="plan",
    )(route)
    pos = pos2[:, 0:2].astype(i32).reshape(-1)
    tile_expert = te[:N_ETILES, 0].astype(i32)
    n_active = misc[0, MOE_EXPERTS - 1].astype(i32).reshape(1)
    last_tile = misc[1, :MOE_EXPERTS].astype(i32)
    return pos, tile_expert, n_active, last_tile


def _dispatch_kernel(pos_ref, last_ref, h2_ref, xs_hbm, zbuf, sem):
    i = pl.program_id(0)

    @pl.when(i == 0)
    def _():
        zbuf[...] = jnp.zeros(zbuf.shape, f32)
        for wait in (False, True):
            for e in range(MOE_EXPERTS):
                @pl.when(last_ref[e] >= 0)
                def _():
                    cp = pltpu.make_async_copy(zbuf, _rt_rows(xs_hbm, last_ref[e], TME), sem)
                    cp.wait() if wait else cp.start()

    def issue(r, carry):
        p = (i * DISP_TM + r) * 2
        for k in range(2):
            pltpu.make_async_copy(_rt_rows(h2_ref, r), _rt_rows(xs_hbm, pos_ref[p + k]), sem).start(priority=k)
        return carry

    lax.fori_loop(0, DISP_TM, issue, 0, unroll=8)
    for k in range(2):
        pltpu.make_async_copy(h2_ref, _rt_rows(xs_hbm, 0, DISP_TM), sem).wait()


def _dispatch(pos, last_tile, h2):
    return pl.pallas_call(
        _dispatch_kernel,
        grid_spec=pltpu.PrefetchScalarGridSpec(
            num_scalar_prefetch=2,
            grid=(T_ALL // DISP_TM,),
            in_specs=[pl.BlockSpec(_rt_shape(DISP_TM), lambda i, pos, last: (i, 0))],
            out_specs=pl.BlockSpec(memory_space=pl.ANY),
            scratch_shapes=[pltpu.VMEM(_rt_shape(TME), f32), pltpu.SemaphoreType.DMA(())],
        ),
        out_shape=jax.ShapeDtypeStruct(_rt_shape(P_PAD), f32),
        compiler_params=_dma_cparams(),
        name="dispatch",
    )(pos, last_tile, h2)


def _moe_kernel(te_ref, na_ref, x_ref, w1_ref, w3_ref, w2_ref, o_ref, w1_scr, w3_scr, w2_scr):
    i = pl.program_id(0)
    changed = (i == 0) | (te_ref[i] != te_ref[jnp.maximum(i - 1, 0)])

    @pl.when(changed)
    def _():
        w1_scr[...] = w1_ref[...].astype(bf16)
        w3_scr[...] = w3_ref[...].astype(bf16)
        w2_scr[...] = w2_ref[...].astype(bf16)

    @pl.when(i < na_ref[0])
    def _():
        x = _from_row_tiles(x_ref).astype(bf16)
        hid = _silu(_dot(x, w1_scr[...])) * _dot(x, w3_scr[...])
        _to_row_tiles(o_ref, _dot(hid.astype(bf16), w2_scr[...]))

    @pl.when(i >= na_ref[0])
    def _():
        o_ref[...] = jnp.zeros(o_ref.shape, f32)


def _moe(tile_expert, n_active, x_sorted, w1, w3, w2, layer):
    wspec = lambda shape: pl.BlockSpec((None, None) + shape, lambda i, te, na: (layer, te[i], 0, 0))
    xmap = lambda i, te, na: (jnp.maximum(jnp.minimum(i, na[0] - 1), 0), 0)
    return pl.pallas_call(
        _moe_kernel,
        grid_spec=pltpu.PrefetchScalarGridSpec(
            num_scalar_prefetch=2,
            grid=(N_ETILES,),
            in_specs=[
                pl.BlockSpec(_rt_shape(TME), xmap),
                wspec((D_MODEL, MOE_D_FF)), wspec((D_MODEL, MOE_D_FF)), wspec((MOE_D_FF, D_MODEL)),
            ],
            out_specs=pl.BlockSpec(_rt_shape(TME), lambda i, te, na: (i, 0)),
            scratch_shapes=[pltpu.VMEM((D_MODEL, MOE_D_FF), bf16), pltpu.VMEM((D_MODEL, MOE_D_FF), bf16),
                            pltpu.VMEM((MOE_D_FF, D_MODEL), bf16)],
        ),
        out_shape=jax.ShapeDtypeStruct(_rt_shape(P_PAD), f32),
        compiler_params=_cparams(1),
        name="moe",
    )(tile_expert, n_active, x_sorted, w1, w3, w2)


def _combine_kernel(pos_ref, x1_ref, route_ref, mod_ref, lng_ref, lnb_ref, y_hbm, oc_ref, od_ref, buf, sem):
    i = pl.program_id(0)
    slot = i % 2

    def gather_tile(tile, into):
        def issue(r, carry):
            p = (tile * TM + r) * 2
            for k in range(2):
                pltpu.make_async_copy(_rt_rows(y_hbm, pos_ref[p + k]), _rt_rows(buf.at[into, k], r),
                                      sem.at[into]).start(priority=k)
            return carry

        lax.fori_loop(0, TM, issue, 0, unroll=8)

    @pl.when(i == 0)
    def _():
        gather_tile(0, 0)

    @pl.when(i + 1 < pl.num_programs(0))
    def _():
        gather_tile(i + 1, 1 - slot)

    m = mod_ref[pl.ds(_row_type(i, TM), 1), :]
    g2 = m[:, 5 * D_MODEL:6 * D_MODEL]
    route = route_ref[...]
    for k in range(2):
        pltpu.make_async_copy(_rt_rows(y_hbm, 0, TM), buf.at[slot, k], sem.at[slot]).wait()
    ff = _from_row_tiles(buf.at[slot, 0]) * route[:, 2:3] + _from_row_tiles(buf.at[slot, 1]) * route[:, 3:4]
    out = _layer_norm(DEEPNORM_ALPHA * x1_ref[...] + g2 * ff, lng_ref[...], lnb_ref[...])

    @pl.when(i < N_CTX_TILES)
    def _():
        oc_ref[...] = out

    @pl.when(i >= N_CTX_TILES)
    def _():
        od_ref[...] = out


def _combine(pos, x1, route, mod, ln_g, ln_b, y_sorted):
    const = lambda shape: pl.BlockSpec(shape, lambda i, pos: (0,) * len(shape))
    row = lambda width: pl.BlockSpec((TM, width), lambda i, pos: (i, 0))
    return pl.pallas_call(
        _combine_kernel,
        grid_spec=pltpu.PrefetchScalarGridSpec(
            num_scalar_prefetch=1,
            grid=(T_ALL // TM,),
            in_specs=[row(D_MODEL), row(LANES), const(mod.shape), const(ln_g.shape), const(ln_b.shape),
                      pl.BlockSpec(memory_space=pl.ANY)],
            out_specs=_x_specs(),
            scratch_shapes=[pltpu.VMEM((2, 2) + _rt_shape(TM), f32), pltpu.SemaphoreType.DMA((2,))],
        ),
        out_shape=[jax.ShapeDtypeStruct((T_CTX, D_MODEL), f32), jax.ShapeDtypeStruct((T_DN, D_MODEL), f32)],
        compiler_params=_dma_cparams(),
        name="combine",
    )(pos, x1, route, mod, ln_g, ln_b, y_sorted)


def _pad_lanes(row):
    return jnp.pad(row, ((0, 0), (0, LANES - row.shape[1])))


def kernel(x_prompt, x_sample, cache_k, cache_v, state_ssm, c, c_ctx, w_ada, b_ada, w_in, na_rpb, ssm_conv_w, ssm_conv_b, ssm_a_log, ssm_dt_bias, ssm_d, ssm_norm_g, gmlp_ln_g, gmlp_ln_b, gmlp_ws, gmlp_bs, w_branch, w_o, ln_g, ln_b, moe_w_coarse, moe_w_fine, moe_w1, moe_w3, moe_w2):
    cvec = jnp.concatenate([c_ctx[None], c, jnp.zeros((N_ROWTYPES - 1 - DEC_BATCH, D_MODEL), f32)], axis=0)
    mods = _mods(cvec, w_ada, b_ada)
    cos, sin = _rope_tables()
    xc, xd = x_prompt.reshape(T_CTX, D_MODEL), x_sample.reshape(T_DN, D_MODEL)
    w_pad = jnp.pad(w_in, ((0, 0), (0, 0), (0, W_PAD - N_IN)))
    kc = vc = st = None
    for l in range(DEPTH):
        mod = mods[l]
        proj = _inproj(xc, xd, mod, w_pad, l)

        ya, kc, vc = _ctx_attn(proj, l, kc, vc)
        ya = _na_attn(proj, cache_k, cache_v, _na_bias(na_rpb[l]), cos, sin, l, ya)

        ssd_params = (
            ssm_conv_w[l], ssm_conv_b[l][None],
            _pad_lanes(ssm_a_log[l].reshape(1, -1)), _pad_lanes(ssm_dt_bias[l].reshape(1, -1)),
            jnp.repeat(ssm_d[l], SSM_HEAD_DIM)[None], ssm_norm_g[l][None],
        )
        if st is None:
            ys, st = _ssd(proj, ssd_params, l, latent=False)
        else:
            ys, st = _ssd(proj, ssd_params, l, latent=False, st_prev=st)
        (ys,) = _ssd(proj, ssd_params, l, latent=True, h0=state_ssm, y_prev=ys)

        bs_full = jnp.repeat(gmlp_bs[l].T, GMLP_WIDTH // GMLP_GROUPS, axis=1)
        yg = _gmlp(proj, gmlp_ln_g[l][None], gmlp_ln_b[l][None], gmlp_ws[l], bs_full)

        w_r = _pad_lanes(jnp.concatenate([moe_w_coarse[l], moe_w_fine[l]], axis=1))
        w_r_hi = w_r.astype(bf16)
        w_r = jnp.concatenate([w_r_hi, (w_r - w_r_hi.astype(f32)).astype(bf16)], axis=1)
        x1, h2, route = _merge(xc, xd, ya, ys, yg, mod, w_pad, w_branch[l].astype(bf16), w_o[l].astype(bf16),
                               ln_g[l, 0][None], ln_b[l, 0][None], w_r, l)

        pos, tile_expert, n_active, last_tile = _plan(route)
        x_sorted = _dispatch(pos, last_tile, h2)
        y_sorted = _moe(tile_expert, n_active, x_sorted, moe_w1, moe_w3, moe_w2, l)
        xc, xd = _combine(pos, x1, route, mod, ln_g[l, 1][None], ln_b[l, 1][None], y_sorted)

    return (xc.reshape(BATCH, SEQ, D_MODEL), xd.reshape(DEC_BATCH, DEC_SEQ, D_MODEL), kc, vc, st)
```

```python
import functools
import math

import numpy as np
import jax
import jax.numpy as jnp
from jax import lax
from jax.experimental import pallas as pl
from jax.experimental.pallas import tpu as pltpu

f32 = jnp.float32
bf16 = jnp.bfloat16
i32 = jnp.int32

D_MODEL = 1024
BATCH = 32
SEQ = 256
DEPTH = 2
DEC_BATCH = 2
DEC_SEQ = 1024
PAST_LEN = 256
GRID_W = 64
NA_HEADS = 8
NA_HEAD_DIM = 64
NA_WIDTH = NA_HEADS * NA_HEAD_DIM
NA_WIN_ROWS = 8
NA_WIN_COLS = 16
ROPE_BASE = 10000.0
SSM_HEADS = 8
SSM_HEAD_DIM = 64
SSM_INNER = SSM_HEADS * SSM_HEAD_DIM
SSM_GROUPS = 2
SSM_STATE = 64
SSM_CONV_DIM = SSM_INNER + 2 * SSM_GROUPS * SSM_STATE
CHUNK = 128
GMLP_GROUPS = 4
GMLP_WIDTH = 512
N_BRANCHES = 3
MOE_GROUPS = 4
MOE_EPG = 8
MOE_EXPERTS = MOE_GROUPS * MOE_EPG
MOE_D_FF = 256
DEEPNORM_ALPHA = (2 * DEPTH) ** 0.25
LN_EPS = 1e-5
RMS_EPS = 1e-5

OFF_Q = 0
OFF_Z = 3 * NA_WIDTH
OFF_XBC = OFF_Z + SSM_INNER
OFF_DT = OFF_XBC + SSM_CONV_DIM
OFF_U = OFF_DT + 2 * SSM_HEADS
OFF_GATE = OFF_U + 2 * GMLP_WIDTH
N_IN = OFF_GATE + N_BRANCHES * D_MODEL

LANES = 128
P_Q, P_K, P_V, P_Z, P_X, P_U, P_GV = 0, 512, 1024, 1536, 2048, 2560, 3072
P_B, P_C, P_DT = 3584, 3712, 3840
N_PROJ = 3968

T_CTX = BATCH * SEQ
T_DN = DEC_BATCH * DEC_SEQ
T_ALL = T_CTX + T_DN
N_ROWTYPES = 8

TM = 256
TME = 256
N_PAIRS = 2 * T_ALL
N_ETILES = (N_PAIRS + MOE_EXPERTS * (TME - 1) + TME - 1) // TME
P_PAD = N_ETILES * TME
PLAN_TM = 1024
TE_ROWS = 128
DISP_TM = 1024

VMEM_LIMIT = 56 * 1024 * 1024


def _cparams(n_axes):
    return pltpu.CompilerParams(dimension_semantics=("arbitrary",) * n_axes, vmem_limit_bytes=VMEM_LIMIT)


def _dma_cparams():
    return pltpu.CompilerParams(dimension_semantics=("arbitrary",), vmem_limit_bytes=VMEM_LIMIT,
                                disable_bounds_checks=True)


def _row_type(tile, rows_per_tile):
    start = tile * rows_per_tile
    return jnp.where(start < T_CTX, 0, 1 + (start - T_CTX) // DEC_SEQ)


def _silu(x):
    return x * jax.nn.sigmoid(x)


def _gelu(x):
    return 0.5 * x * (1.0 + lax.erf(x * (1.0 / math.sqrt(2.0))))


def _dot(a, b):
    return jnp.dot(a, b, preferred_element_type=f32)


def _dot_nt(a, b):
    return lax.dot_general(a, b, (((1,), (1,)), ((), ())), preferred_element_type=f32)


def _dot_tn(a, b):
    return lax.dot_general(a, b, (((0,), (0,)), ((), ())), preferred_element_type=f32)


RT = D_MODEL // LANES


def _rt_shape(n):
    return (n * RT, LANES)


def _rt_rows(ref, start, n=1):
    return ref.at[pl.ds(pl.multiple_of(start * RT, RT), n * RT)]


def _to_row_tiles(ref, value):
    n = value.shape[0]
    for s in range(RT):
        ref[pl.ds(s, n, stride=RT), :] = value[:, s * LANES:(s + 1) * LANES]


def _from_row_tiles(ref):
    n = ref.shape[0] // RT
    return jnp.concatenate([ref[pl.ds(s, n, stride=RT), :] for s in range(RT)], axis=1)


W_PAD = 2 * 3840
GATE_BLOCK = 3840
TAIL_BLOCK = 1408
LANE_SHIFT = OFF_U % LANES
N_CTX_TILES = T_CTX // TM


def _unshift_columns(w, n_out):
    lane = lax.broadcasted_iota(i32, (w.shape[0], LANES), 1)
    rolled = [pltpu.roll(w[:, b * LANES:(b + 1) * LANES], LANES - LANE_SHIFT, 1) for b in range(n_out + 1)]
    return [jnp.where(lane < LANES - LANE_SHIFT, rolled[b], rolled[b + 1]).astype(bf16) for b in range(n_out)]


def _x_specs(imap=lambda i: i):
    ctx = pl.BlockSpec((TM, D_MODEL), lambda i, *_: (jnp.minimum(imap(i), N_CTX_TILES - 1), 0))
    lat = pl.BlockSpec((TM, D_MODEL), lambda i, *_: (jnp.maximum(imap(i) - N_CTX_TILES, 0), 0))
    return [ctx, lat]


def _load_x(xc_ref, xd_ref):
    return jnp.where(pl.program_id(0) < N_CTX_TILES, xc_ref[...], xd_ref[...])


def _mods_kernel(c_ref, w_ref, b_ref, o_ref):
    o_ref[...] = _dot(_silu(c_ref[...]).astype(bf16), w_ref[...].astype(bf16)) + b_ref[...]


def _mods(cvec, w_ada, b_ada):
    tn = 1536
    return pl.pallas_call(
        _mods_kernel,
        grid=(DEPTH, 6 * D_MODEL // tn),
        in_specs=[
            pl.BlockSpec((N_ROWTYPES, D_MODEL), lambda l, j: (0, 0)),
            pl.BlockSpec((None, D_MODEL, tn), lambda l, j: (l, 0, j)),
            pl.BlockSpec((None, 1, tn), lambda l, j: (l, 0, j)),
        ],
        out_specs=pl.BlockSpec((None, N_ROWTYPES, tn), lambda l, j: (l, 0, j)),
        out_shape=jax.ShapeDtypeStruct((DEPTH, N_ROWTYPES, 6 * D_MODEL), f32),
        compiler_params=_cparams(2),
        name="mods",
    )(cvec, w_ada, b_ada.reshape(DEPTH, 1, 6 * D_MODEL))


def _inproj_kernel(xc_ref, xd_ref, mod_ref, wh_ref, wt_ref, o_ref, wh_scr, wt_scr):
    @pl.when(pl.program_id(0) == 0)
    def _():
        wh_scr[...] = wh_ref[...].astype(bf16)
        wt = wt_ref[...]
        lane = lax.broadcasted_iota(i32, (D_MODEL, LANES), 1)
        wt_scr[:, 0:LANES] = jnp.where(lane < LANE_SHIFT, wt[:, 0:LANES], 0.0).astype(bf16)
        for b, blk in enumerate(_unshift_columns(wt, 2 * GMLP_WIDTH // LANES)):
            wt_scr[:, (b + 1) * LANES:(b + 2) * LANES] = blk

    m = mod_ref[pl.ds(_row_type(pl.program_id(0), TM), 1), :]
    h = (_load_x(xc_ref, xd_ref) * (1.0 + m[:, D_MODEL:2 * D_MODEL]) + m[:, 0:D_MODEL]).astype(bf16)
    head = _dot(h, wh_scr[...])
    tail = _dot(h, wt_scr[...])
    n_bc = 2 * SSM_GROUPS * SSM_STATE
    o_ref[:, 0:P_U] = head[:, 0:P_U]
    o_ref[:, P_U:P_B] = tail[:, LANES:]
    o_ref[:, P_B:P_B + n_bc] = head[:, P_U:P_U + n_bc]
    o_ref[:, P_DT:N_PROJ] = tail[:, 0:LANES]


def _inproj(xc, xd, mod, w_pad, layer):
    assert OFF_DT % TAIL_BLOCK == 0 and TAIL_BLOCK >= OFF_GATE - OFF_DT + LANES
    return pl.pallas_call(
        _inproj_kernel,
        grid=(T_ALL // TM,),
        in_specs=_x_specs() + [
            pl.BlockSpec((N_ROWTYPES, 6 * D_MODEL), lambda i: (0, 0)),
            pl.BlockSpec((None, D_MODEL, OFF_DT), lambda i: (layer, 0, 0), pipeline_mode=pl.Buffered(1)),
            pl.BlockSpec((None, D_MODEL, TAIL_BLOCK), lambda i: (layer, 0, OFF_DT // TAIL_BLOCK),
                         pipeline_mode=pl.Buffered(1)),
        ],
        out_specs=pl.BlockSpec((TM, N_PROJ), lambda i: (i, 0)),
        out_shape=jax.ShapeDtypeStruct((T_ALL, N_PROJ), f32),
        scratch_shapes=[pltpu.VMEM((D_MODEL, OFF_DT), bf16), pltpu.VMEM((D_MODEL, LANES + 2 * GMLP_WIDTH), bf16)],
        compiler_params=_cparams(1),
        name="inproj",
    )(xc, xd, mod, w_pad, w_pad)


def _ctx_attn_kernel(q_ref, k_ref, v_ref, y_ref, kc_ref, vc_ref):
    scale = NA_HEAD_DIM ** -0.5
    outs = []
    for h in range(NA_HEADS):
        sl = slice(h * NA_HEAD_DIM, (h + 1) * NA_HEAD_DIM)
        q, k, v = q_ref[:, sl], k_ref[:, sl], v_ref[:, sl]
        kc_ref[h] = k
        vc_ref[h] = v
        s = _dot_nt(q.astype(bf16), k.astype(bf16)) * scale
        e = jnp.exp(s - jnp.max(s, -1, keepdims=True))
        p = e / jnp.sum(e, -1, keepdims=True)
        outs.append(_dot(p.astype(bf16), v.astype(bf16)))
    y_ref[...] = jnp.concatenate(outs, axis=1)


def _ctx_attn(proj, layer, kc_prev, vc_prev):
    cache_shape = jax.ShapeDtypeStruct((BATCH, DEPTH, NA_HEADS, SEQ, NA_HEAD_DIM), f32)
    in_specs = [
        pl.BlockSpec((SEQ, NA_WIDTH), lambda b: (b, P_Q // NA_WIDTH)),
        pl.BlockSpec((SEQ, NA_WIDTH), lambda b: (b, P_K // NA_WIDTH)),
        pl.BlockSpec((SEQ, NA_WIDTH), lambda b: (b, P_V // NA_WIDTH)),
    ]
    args = [proj, proj, proj]
    aliases = {}
    kernel = _ctx_attn_kernel
    if kc_prev is not None:
        in_specs += [pl.BlockSpec(memory_space=pl.ANY)] * 2
        args += [kc_prev, vc_prev]
        aliases = {3: 1, 4: 2}
        kernel = lambda q, k, v, _kc, _vc, y, kc, vc: _ctx_attn_kernel(q, k, v, y, kc, vc)
    cache_spec = pl.BlockSpec((None, None, NA_HEADS, SEQ, NA_HEAD_DIM), lambda b: (b, layer, 0, 0, 0))
    return pl.pallas_call(
        kernel,
        grid=(BATCH,),
        in_specs=in_specs,
        out_specs=[pl.BlockSpec((SEQ, NA_WIDTH), lambda b: (b, 0)), cache_spec, cache_spec],
        out_shape=[jax.ShapeDtypeStruct((T_ALL, NA_WIDTH), f32), cache_shape, cache_shape],
        input_output_aliases=aliases,
        compiler_params=_cparams(1),
        name="ctx_attn",
    )(*args)


GRID_ROWS = DEC_SEQ // GRID_W
N_WIN = NA_WIN_ROWS * GRID_W


def _rope_tables():
    quarter = NA_HEAD_DIM // 4
    freqs = ROPE_BASE ** (-np.arange(quarter, dtype=np.float32) / quarter)
    lane = np.arange(LANES)
    j = lane % NA_HEAD_DIM
    use_col = (j // (NA_HEAD_DIM // 2)) == 1
    fi = j % quarter
    second = (j % (NA_HEAD_DIM // 2)) >= quarter
    pos = jnp.arange(DEC_SEQ)
    p = jnp.where(use_col[None, :], (pos % GRID_W)[:, None], (pos // GRID_W)[:, None]).astype(f32)
    ang = p * jnp.asarray(freqs[fi], f32)[None, :]
    cos = jnp.cos(ang)
    sin = jnp.sin(ang)
    return cos, jnp.where(second[None, :], sin, -sin)


def _na_bias(rpb):
    qc = np.arange(GRID_W)[:, None]
    kc = np.arange(GRID_W)[None, :]
    col0 = np.clip(qc - NA_WIN_COLS // 2, 0, GRID_W - NA_WIN_COLS)
    valid = (kc >= col0) & (kc < col0 + NA_WIN_COLS)
    dc = np.clip(kc - qc + NA_WIN_COLS - 1, 0, 2 * NA_WIN_COLS - 2)
    onehot = (dc[None] == np.arange(2 * NA_WIN_COLS - 1)[:, None, None]).astype(np.float32)
    picked = jnp.einsum('hdr,rqk->hdqk', rpb.astype(f32), jnp.asarray(onehot), precision=lax.Precision.HIGHEST)
    tb = jnp.where(valid[None, None], picked, -jnp.inf)
    rows = [jnp.concatenate([tb[:, d0 + i] for i in range(NA_WIN_ROWS)], axis=-1) for d0 in range(NA_WIN_ROWS)]
    return jnp.stack(rows, axis=1)


def _rope(x, cos, sin_signed):
    lane = lax.broadcasted_iota(i32, x.shape, 1)
    first = (lane % (NA_HEAD_DIM // 2)) < (NA_HEAD_DIM // 4)
    q = NA_HEAD_DIM // 4
    partner = jnp.where(first, pltpu.roll(x, LANES - q, 1), pltpu.roll(x, q, 1))
    return x * cos + partner * sin_signed


def _na_attn_kernel(q_ref, k_ref, v_ref, kctx_ref, vctx_ref, bias_ref, cos_ref, sin_ref, _y_in, y_ref,
                    q_scr, k_scr, v_scr, sc_scr, y_scr):
    scale = NA_HEAD_DIM ** -0.5
    cos, sin = cos_ref[...], sin_ref[...]
    q = _rope(q_ref[...], cos, sin).astype(bf16)
    k = _rope(k_ref[...], cos, sin).astype(bf16)
    v = v_ref[...].astype(bf16)
    for hh in range(2):
        sl = slice(hh * NA_HEAD_DIM, (hh + 1) * NA_HEAD_DIM)
        q_scr[hh] = q[:, sl]
        k_scr[hh] = k[:, sl]
        v_scr[hh] = v[:, sl]
        sc_scr[hh] = _dot_nt(q[:, sl], kctx_ref[hh].astype(bf16)) * scale
    for hh in range(2):
        vctx = vctx_ref[hh].astype(bf16)

        def row_body(r, carry):
            kr0 = jnp.clip(r - NA_WIN_ROWS // 2, 0, GRID_ROWS - NA_WIN_ROWS)
            d0 = kr0 - r + NA_WIN_ROWS - 1
            q0 = pl.multiple_of(r * GRID_W, GRID_W)
            k0 = pl.multiple_of(kr0 * GRID_W, GRID_W)
            qr = q_scr[hh, pl.ds(q0, GRID_W), :]
            kw = k_scr[hh, pl.ds(k0, N_WIN), :]
            vw = v_scr[hh, pl.ds(k0, N_WIN), :]
            s_win = _dot_nt(qr, kw) * scale + bias_ref[hh, d0]
            s_ctx = sc_scr[hh, pl.ds(q0, GRID_W), :]
            m = jnp.maximum(jnp.max(s_win, -1, keepdims=True), jnp.max(s_ctx, -1, keepdims=True))
            e_win = jnp.exp(s_win - m)
            e_ctx = jnp.exp(s_ctx - m)
            denom = jnp.sum(e_win, -1, keepdims=True) + jnp.sum(e_ctx, -1, keepdims=True)
            o = _dot(e_win.astype(bf16), vw) + _dot(e_ctx.astype(bf16), vctx)
            y_scr[hh, pl.ds(q0, GRID_W), :] = o / denom
            return carry

        lax.fori_loop(0, GRID_ROWS, row_body, 0)
    y_ref[...] = jnp.concatenate([y_scr[0], y_scr[1]], axis=1)


def _na_attn(proj, cache_k, cache_v, bias, cos, sin, layer, y_att):
    dn0 = T_CTX // DEC_SEQ
    blk = lambda off: pl.BlockSpec((DEC_SEQ, LANES), lambda b, hp: (dn0 + b, off // LANES + hp))
    ctx_spec = pl.BlockSpec((None, None, 2, PAST_LEN, NA_HEAD_DIM), lambda b, hp: (b, layer, hp, 0, 0))
    return pl.pallas_call(
        _na_attn_kernel,
        grid=(DEC_BATCH, NA_HEADS // 2),
        in_specs=[
            blk(P_Q), blk(P_K), blk(P_V), ctx_spec, ctx_spec,
            pl.BlockSpec((2, NA_WIN_ROWS, GRID_W, N_WIN), lambda b, hp: (hp, 0, 0, 0)),
            pl.BlockSpec((DEC_SEQ, LANES), lambda b, hp: (0, 0)),
            pl.BlockSpec((DEC_SEQ, LANES), lambda b, hp: (0, 0)),
            pl.BlockSpec(memory_space=pl.ANY),
        ],
        out_specs=pl.BlockSpec((DEC_SEQ, LANES), lambda b, hp: (dn0 + b, hp)),
        out_shape=jax.ShapeDtypeStruct((T_ALL, NA_WIDTH), f32),
        scratch_shapes=[
            pltpu.VMEM((2, DEC_SEQ, NA_HEAD_DIM), bf16),
            pltpu.VMEM((2, DEC_SEQ, NA_HEAD_DIM), bf16),
            pltpu.VMEM((2, DEC_SEQ, NA_HEAD_DIM), bf16),
            pltpu.VMEM((2, DEC_SEQ, PAST_LEN), f32),
            pltpu.VMEM((2, DEC_SEQ, NA_HEAD_DIM), f32),
        ],
        input_output_aliases={8: 0},
        compiler_params=_cparams(2),
        name="na_attn",
    )(proj, proj, proj, cache_k, cache_v, bias, cos, sin, y_att)


def _conv_silu(v, w, b):
    n = v.shape[0]
    rows = lax.broadcasted_iota(i32, v.shape, 0)
    prev = jnp.where(rows == 0, 0.0, pltpu.roll(v, 1, 0))
    nxt = jnp.where(rows == n - 1, 0.0, pltpu.roll(v, n - 1, 0))
    return _silu(prev * w[0:1] + v * w[1:2] + nxt * w[2:3] + b)


def _ssd_kernel(*refs, seq_len, has_h0, emit_state, aliased):
    refs = list(refs)
    z_ref, x_ref, b_ref, c_ref, dt_ref, cw_ref, cb_ref, alog_ref, dtb_ref, dskip_ref, ng_ref = refs[:11]
    pos = 11
    h0_ref = None
    if has_h0:
        h0_ref = refs[pos]
        pos += 1
    pos += aliased
    y_ref = refs[pos]
    pos += 1
    st_ref = None
    if emit_state:
        st_ref = refs[pos]
        pos += 1
    xs_scr, b_scr, c_scr, dt_scr, dta_scr, y_scr, h_scr = refs[pos:]

    n_chunks = seq_len // CHUNK
    cw = cw_ref[...]
    cb = cb_ref[...]
    xs = _conv_silu(x_ref[...], cw[:, 0:SSM_INNER], cb[:, 0:SSM_INNER])
    xs_scr[...] = xs
    gn = SSM_GROUPS * SSM_STATE
    b_scr[...] = _conv_silu(b_ref[...], cw[:, SSM_INNER:SSM_INNER + gn], cb[:, SSM_INNER:SSM_INNER + gn])
    c_scr[...] = _conv_silu(c_ref[...], cw[:, SSM_INNER + gn:], cb[:, SSM_INNER + gn:])
    raw = dt_ref[...] + dtb_ref[...]
    dt = jnp.maximum(raw, 0.0) + jnp.log1p(jnp.exp(-jnp.abs(raw)))
    dt_scr[...] = dt
    dta_scr[...] = dt * (-jnp.exp(alog_ref[...]))
    y_scr[...] = xs * dskip_ref[...]
    rep = SSM_HEADS // SSM_GROUPS
    head_rows = lambda hh: slice(hh * SSM_HEAD_DIM, (hh + 1) * SSM_HEAD_DIM)
    if has_h0:
        for d in range(2):
            for h in range(SSM_HEADS):
                h_scr[d, h // rep, head_rows(h % rep), :] = h0_ref[d, h]
    else:
        h_scr[...] = jnp.zeros(h_scr.shape, f32)

    ri = lax.broadcasted_iota(i32, (CHUNK, CHUNK), 0)
    ci = lax.broadcasted_iota(i32, (CHUNK, CHUNK), 1)
    keep = (ri >= ci, ri <= ci)
    def one_chunk(direction, chunk):
        r0 = pl.multiple_of(chunk * CHUNK, CHUNK)
        rows = pl.ds(r0, CHUNK)
        mask = keep[direction]
        cum = jnp.dot(mask.astype(f32), dta_scr[rows, :], precision=lax.Precision.HIGHEST,
                      preferred_element_type=f32)
        cum_t = cum.T
        end_row = CHUNK - 1 if direction == 0 else 0
        chunk_decay = jnp.exp(cum[end_row:end_row + 1, :])
        bc = b_scr[rows, :].astype(bf16)
        cc = c_scr[rows, :].astype(bf16)
        xc = xs_scr[rows, :]
        dtc = dt_scr[rows, :]
        gw = rep * SSM_HEAD_DIM
        head_of_lane = lax.broadcasted_iota(i32, (CHUNK, gw), 1) // SSM_HEAD_DIM
        ys = []
        for g in range(SSM_GROUPS):
            bg = bc[:, g * SSM_STATE:(g + 1) * SSM_STATE]
            cg = cc[:, g * SSM_STATE:(g + 1) * SSM_STATE]
            cb_g = _dot_nt(cg, bg)
            scores, cum_x, dt_x, decay_rows = [], None, None, []
            for hh in range(rep):
                ln = direction * SSM_HEADS + g * rep + hh
                col = jnp.broadcast_to(cum[:, ln:ln + 1], (CHUNK, CHUNK))
                decay = jnp.exp(jnp.where(mask, col - cum_t[ln:ln + 1, :], -jnp.inf))
                scores.append((cb_g * decay).astype(bf16))
                col_w = jnp.concatenate([col] * (gw // CHUNK), axis=1)
                dt_w = jnp.broadcast_to(dtc[:, ln:ln + 1], (CHUNK, gw))
                cum_x = col_w if hh == 0 else jnp.where(head_of_lane == hh, col_w, cum_x)
                dt_x = dt_w if hh == 0 else jnp.where(head_of_lane == hh, dt_w, dt_x)
                decay_rows.append(jnp.broadcast_to(chunk_decay[:, ln:ln + 1], (SSM_HEAD_DIM, SSM_STATE)))
            xdt = xc[:, g * gw:(g + 1) * gw] * dt_x
            to_end = jnp.exp(cum_x[end_row:end_row + 1, :] - cum_x)
            rhs = jnp.concatenate([jnp.where(head_of_lane == hh, xdt, 0.0).astype(bf16) for hh in range(rep)], axis=0)
            h_prev = h_scr[direction, g]
            y = _dot(jnp.concatenate(scores, axis=1), rhs)
            y = y + _dot_nt(cg, h_prev.astype(bf16)) * jnp.exp(cum_x)
            states = _dot_tn((xdt * to_end).astype(bf16), bg)
            h_scr[direction, g] = h_prev * jnp.concatenate(decay_rows, axis=0) + states
            ys.append(y)
        y_scr[rows, :] = y_scr[rows, :] + jnp.concatenate(ys, axis=1)

    def step(s, carry):
        one_chunk(0, s)
        one_chunk(1, n_chunks - 1 - s)
        return carry

    lax.fori_loop(0, n_chunks, step, 0)

    z = z_ref[...]
    y = y_scr[...] * _silu(z)
    y = y * lax.rsqrt(jnp.mean(jnp.square(y), -1, keepdims=True) + RMS_EPS)
    y_ref[...] = y * ng_ref[...]
    if emit_state:
        for d in range(2):
            for h in range(SSM_HEADS):
                st_ref[d, h] = h_scr[d, h // rep, head_rows(h % rep), :]


def _ssd(proj, params, layer, *, latent, h0=None, y_prev=None, st_prev=None):
    seq_len = DEC_SEQ if latent else SEQ
    n_seq = DEC_BATCH if latent else BATCH
    blk0 = (T_CTX // DEC_SEQ) if latent else 0
    blk = lambda width, off: pl.BlockSpec((seq_len, width), lambda b: (blk0 + b, off // width))
    const = lambda shape: pl.BlockSpec(shape, lambda b: (0,) * len(shape))
    conv_w, conv_b, a_log_row, dt_bias_row, d_skip_row, norm_g = params
    in_specs = [
        blk(SSM_INNER, P_Z), blk(SSM_INNER, P_X), blk(LANES, P_B), blk(LANES, P_C), blk(LANES, P_DT),
        const(conv_w.shape), const(conv_b.shape), const(a_log_row.shape), const(dt_bias_row.shape),
        const(d_skip_row.shape), const(norm_g.shape),
    ]
    args = [proj, proj, proj, proj, proj, conv_w, conv_b, a_log_row, dt_bias_row, d_skip_row, norm_g]
    state_block = (None, None, 2, SSM_HEADS, SSM_HEAD_DIM, SSM_STATE)
    if h0 is not None:
        in_specs.append(pl.BlockSpec(state_block, lambda b: (b, layer, 0, 0, 0, 0)))
        args.append(h0)
    aliases = {}
    n_alias = 0
    out_specs = [pl.BlockSpec((seq_len, SSM_INNER), lambda b: (blk0 + b, 0))]
    out_shape = [jax.ShapeDtypeStruct((T_ALL, SSM_INNER), f32)]
    if y_prev is not None:
        aliases[len(args)] = 0
        in_specs.append(pl.BlockSpec(memory_space=pl.ANY))
        args.append(y_prev)
        n_alias += 1
    emit_state = not latent
    if emit_state:
        out_specs.append(pl.BlockSpec(state_block, lambda b: (b, layer, 0, 0, 0, 0)))
        out_shape.append(jax.ShapeDtypeStruct((BATCH, DEPTH, 2, SSM_HEADS, SSM_HEAD_DIM, SSM_STATE), f32))
        if st_prev is not None:
            aliases[len(args)] = 1
            in_specs.append(pl.BlockSpec(memory_space=pl.ANY))
            args.append(st_prev)
            n_alias += 1
    return pl.pallas_call(
        functools.partial(_ssd_kernel, seq_len=seq_len, has_h0=h0 is not None, emit_state=emit_state,
                          aliased=n_alias),
        grid=(n_seq,),
        in_specs=in_specs,
        out_specs=out_specs,
        out_shape=out_shape,
        scratch_shapes=[
            pltpu.VMEM((seq_len, SSM_INNER), f32),
            pltpu.VMEM((seq_len, LANES), f32),
            pltpu.VMEM((seq_len, LANES), f32),
            pltpu.VMEM((seq_len, LANES), f32),
            pltpu.VMEM((seq_len, LANES), f32),
            pltpu.VMEM((seq_len, SSM_INNER), f32),
            pltpu.VMEM((2, SSM_GROUPS, SSM_HEADS // SSM_GROUPS * SSM_HEAD_DIM, SSM_STATE), f32),
        ],
        input_output_aliases=aliases,
        compiler_params=_cparams(1),
        name="ssd_latent" if latent else "ssd_ctx",
    )(*args)


def _gmlp_kernel(u_ref, v_ref, g_ref, b_ref, ws_ref, bs_ref, o_ref):
    u = _gelu(u_ref[...])
    v = _gelu(v_ref[...])
    mu = jnp.mean(v, -1, keepdims=True)
    var = jnp.mean(jnp.square(v - mu), -1, keepdims=True)
    v = ((v - mu) * lax.rsqrt(var + LN_EPS) * g_ref[...] + b_ref[...]).astype(bf16)
    gd = GMLP_WIDTH // GMLP_GROUPS
    rows = []
    for c in range(TM // CHUNK):
        cols = [_dot(ws_ref[g].astype(bf16), v[c * CHUNK:(c + 1) * CHUNK, g * gd:(g + 1) * gd])
                for g in range(GMLP_GROUPS)]
        rows.append(jnp.concatenate(cols, axis=1) + bs_ref[...])
    o_ref[...] = u * jnp.concatenate(rows, axis=0)


def _gmlp(proj, ln_g, ln_b, ws, bs_full):
    const = lambda shape: pl.BlockSpec(shape, lambda i: (0,) * len(shape))
    return pl.pallas_call(
        _gmlp_kernel,
        grid=(T_ALL // TM,),
        in_specs=[
            pl.BlockSpec((TM, GMLP_WIDTH), lambda i: (i, P_U // GMLP_WIDTH)),
            pl.BlockSpec((TM, GMLP_WIDTH), lambda i: (i, P_GV // GMLP_WIDTH)),
            const(ln_g.shape), const(ln_b.shape), const(ws.shape), const(bs_full.shape),
        ],
        out_specs=pl.BlockSpec((TM, GMLP_WIDTH), lambda i: (i, 0)),
        out_shape=jax.ShapeDtypeStruct((T_ALL, GMLP_WIDTH), f32),
        compiler_params=_cparams(1),
        name="gmlp",
    )(proj, proj, ln_g, ln_b, ws, bs_full)


def _layer_norm(x, g, b):
    mu = jnp.mean(x, -1, keepdims=True)
    var = jnp.mean(jnp.square(x - mu), -1, keepdims=True)
    return (x - mu) * lax.rsqrt(var + LN_EPS) * g + b


def _route(logits):
    lane = lax.broadcasted_iota(i32, logits.shape, 1).astype(f32)
    big = 1e9
    lc = jnp.where(lane < MOE_GROUPS, logits, -jnp.inf)
    mc = jnp.max(lc, -1, keepdims=True)
    p_g = 1.0 / jnp.sum(jnp.exp(lc - mc), -1, keepdims=True)
    g_sel = jnp.min(jnp.where(lc == mc, lane, big), -1, keepdims=True)
    lo = MOE_GROUPS + g_sel * MOE_EPG
    fm = jnp.where((lane >= lo) & (lane < lo + MOE_EPG), logits, -jnp.inf)
    v1 = jnp.max(fm, -1, keepdims=True)
    i1 = jnp.min(jnp.where(fm == v1, lane, big), -1, keepdims=True)
    fm2 = jnp.where(lane == i1, -jnp.inf, fm)
    v2 = jnp.max(fm2, -1, keepdims=True)
    i2 = jnp.min(jnp.where(fm2 == v2, lane, big), -1, keepdims=True)
    e2 = jnp.exp(v2 - v1)
    w1 = p_g / (1.0 + e2)
    w2 = p_g * e2 / (1.0 + e2)
    out = jnp.where(lane == 0, i1 - MOE_GROUPS, 0.0)
    out = jnp.where(lane == 1, i2 - MOE_GROUPS, out)
    out = jnp.where(lane == 2, w1, out)
    return jnp.where(lane == 3, w2, out)


def _merge_kernel(xc_ref, xd_ref, ya_ref, ys_ref, yg_ref, mod_ref, wg_ref, wb_ref, wo_ref, lng_ref, lnb_ref, wr_ref,
                  x1_ref, h2_ref, route_ref, wg_scr):
    @pl.when(pl.program_id(0) == 0)
    def _():
        for b, blk in enumerate(_unshift_columns(wg_ref[...], N_BRANCHES * D_MODEL // LANES)):
            wg_scr[:, b * LANES:(b + 1) * LANES] = blk

    m = mod_ref[pl.ds(_row_type(pl.program_id(0), TM), 1), :]
    sh1, sc1, g1 = m[:, 0:D_MODEL], m[:, D_MODEL:2 * D_MODEL], m[:, 2 * D_MODEL:3 * D_MODEL]
    sh2, sc2 = m[:, 3 * D_MODEL:4 * D_MODEL], m[:, 4 * D_MODEL:5 * D_MODEL]
    x = _load_x(xc_ref, xd_ref)
    h = (x * (1.0 + sc1) + sh1).astype(bf16)
    merged = None
    for i, y_ref in enumerate((ya_ref, ys_ref, yg_ref)):
        gate = jax.nn.sigmoid(_dot(h, wg_scr[:, i * D_MODEL:(i + 1) * D_MODEL]))
        term = gate * _dot(y_ref[...].astype(bf16), wb_ref[i])
        merged = term if merged is None else merged + term
    out = _dot(merged.astype(bf16), wo_ref[...])
    x1 = _layer_norm(DEEPNORM_ALPHA * x + g1 * out, lng_ref[...], lnb_ref[...])
    x1_ref[...] = x1
    h2 = x1 * (1.0 + sc2) + sh2
    _to_row_tiles(h2_ref, h2)
    h2_hi = h2.astype(bf16)
    h2_lo = (h2 - h2_hi.astype(f32)).astype(bf16)
    parts = _dot(h2_hi, wr_ref[...]) + _dot(h2_lo, wr_ref[...])
    logits = parts[:, 0:LANES] + parts[:, LANES:2 * LANES]
    route_ref[...] = _route(logits)


def _merge(xc, xd, ya, ys, yg, mod, w_pad, w_br, w_o, ln_g, ln_b, w_r, layer):
    assert OFF_GATE // GATE_BLOCK == 1 and OFF_GATE % GATE_BLOCK == LANE_SHIFT
    const = lambda shape: pl.BlockSpec(shape, lambda i: (0,) * len(shape))
    row = lambda width: pl.BlockSpec((TM, width), lambda i: (i, 0))
    return pl.pallas_call(
        _merge_kernel,
        grid=(T_ALL // TM,),
        in_specs=_x_specs() + [
            row(NA_WIDTH), row(SSM_INNER), row(GMLP_WIDTH), const(mod.shape),
            pl.BlockSpec((None, D_MODEL, GATE_BLOCK), lambda i: (layer, 0, 1), pipeline_mode=pl.Buffered(1)),
            const(w_br.shape), const(w_o.shape), const(ln_g.shape), const(ln_b.shape),
            const(w_r.shape),
        ],
        out_specs=[row(D_MODEL), pl.BlockSpec(_rt_shape(TM), lambda i: (i, 0)), row(LANES)],
        out_shape=[
            jax.ShapeDtypeStruct((T_ALL, D_MODEL), f32),
            jax.ShapeDtypeStruct(_rt_shape(T_ALL), f32),
            jax.ShapeDtypeStruct((T_ALL, LANES), f32),
        ],
        scratch_shapes=[pltpu.VMEM((D_MODEL, N_BRANCHES * D_MODEL), bf16)],
        compiler_params=_cparams(1),
        name="merge",
    )(xc, xd, ya, ys, yg, mod, w_pad, w_br, w_o, ln_g, ln_b, w_r)


def _plan_kernel(route_ref, pos_ref, te_ref, misc_ref, cnt_scr, offs_scr):
    phase, i = pl.program_id(0), pl.program_id(1)
    lane = lax.broadcasted_iota(i32, (PLAN_TM, LANES), 1).astype(f32)
    route = route_ref[...]
    hit0 = lane == route[:, 0:1]
    hit1 = lane == route[:, 1:2]
    onehot = jnp.where(hit0 | hit1, 1.0, 0.0)
    tile_counts = jnp.sum(onehot, axis=0, keepdims=True)

    @pl.when((phase == 0) & (i == 0))
    def _():
        cnt_scr[...] = jnp.zeros(cnt_scr.shape, f32)

    @pl.when(phase == 0)
    def _():
        cnt_scr[...] = cnt_scr[...] + tile_counts

    @pl.when((phase == 1) & (i == 0))
    def _():
        lane1 = lax.broadcasted_iota(i32, (8, LANES), 1)
        counts = jnp.broadcast_to(cnt_scr[...], (8, LANES))
        padded = jnp.where(lane1 < MOE_EXPERTS, jnp.ceil(counts * (1.0 / TME)) * TME, 0.0)
        before = (lax.broadcasted_iota(i32, (LANES, LANES), 0) < lax.broadcasted_iota(i32, (LANES, LANES), 1))
        offs = jnp.dot(padded, before.astype(f32), precision=lax.Precision.HIGHEST, preferred_element_type=f32)
        ends = offs + padded
        offs_scr[...] = offs[0:1]
        cnt_scr[...] = jnp.zeros(cnt_scr.shape, f32)
        tile_start = lax.broadcasted_iota(i32, (TE_ROWS, LANES), 0).astype(f32) * TME
        lane2 = lax.broadcasted_iota(i32, (TE_ROWS, LANES), 1)
        passed = jnp.where((lane2 < MOE_EXPERTS) & (tile_start >= ends[0:1]), 1.0, 0.0)
        te = jnp.minimum(jnp.sum(passed, axis=1, keepdims=True), MOE_EXPERTS - 1.0)
        te_ref[...] = jnp.broadcast_to(te, (TE_ROWS, LANES))
        row = lax.broadcasted_iota(i32, (8, LANES), 0)
        last_tile = jnp.where(padded > 0, ends - TME, -1.0)
        misc_ref[...] = jnp.where(row == 0, ends * (1.0 / TME), jnp.where(row == 1, last_tile, 0.0))

    @pl.when(phase == 1)
    def _():
        base = cnt_scr[...]
        ri = lax.broadcasted_iota(i32, (PLAN_TM, PLAN_TM), 0)
        ci = lax.broadcasted_iota(i32, (PLAN_TM, PLAN_TM), 1)
        earlier = jnp.where(ci < ri, 1.0, 0.0).astype(bf16)
        total = _dot(earlier, onehot.astype(bf16)) + (base + offs_scr[...])
        p0 = jnp.sum(jnp.where(hit0, total, 0.0), axis=1, keepdims=True)
        p1 = jnp.sum(jnp.where(hit1, total, 0.0), axis=1, keepdims=True)
        pos_ref[...] = jnp.where(lane == 0, p0, jnp.where(lane == 1, p1, 0.0))
        cnt_scr[...] = base + tile_counts


def _plan(route):
    pos2, te, misc = pl.pallas_call(
        _plan_kernel,
        grid=(2, T_ALL // PLAN_TM),
        in_specs=[pl.BlockSpec((PLAN_TM, LANES), lambda ph, i: (i, 0))],
        out_specs=[
            pl.BlockSpec((PLAN_TM, LANES), lambda ph, i: (ph * i, 0)),
            pl.BlockSpec((TE_ROWS, LANES), lambda ph, i: (0, 0)),
            pl.BlockSpec((8, LANES), lambda ph, i: (0, 0)),
        ],
        out_shape=[
            jax.ShapeDtypeStruct((T_ALL, LANES), f32),
            jax.ShapeDtypeStruct((TE_ROWS, LANES), f32),
            jax.ShapeDtypeStruct((8, LANES), f32),
        ],
        scratch_shapes=[pltpu.VMEM((1, LANES), f32), pltpu.VMEM((1, LANES), f32)],
        compiler_params=_cparams(2),
        name="plan",
    )(route)
    pos = pos2[:, 0:2].astype(i32).reshape(-1)
    tile_expert = te[:N_ETILES, 0].astype(i32)
    n_active = misc[0, MOE_EXPERTS - 1].astype(i32).reshape(1)
    last_tile = misc[1, :MOE_EXPERTS].astype(i32)
    return pos, tile_expert, n_active, last_tile


def _dispatch_kernel(pos_ref, last_ref, h2_ref, xs_hbm, zbuf, sem):
    i = pl.program_id(0)

    @pl.when(i == 0)
    def _():
        zbuf[...] = jnp.zeros(zbuf.shape, f32)
        for wait in (False, True):
            for e in range(MOE_EXPERTS):
                @pl.when(last_ref[e] >= 0)
                def _():
                    cp = pltpu.make_async_copy(zbuf, _rt_rows(xs_hbm, last_ref[e], TME), sem)
                    cp.wait() if wait else cp.start()

    def issue(r, carry):
        p = (i * DISP_TM + r) * 2
        for k in range(2):
            pltpu.make_async_copy(_rt_rows(h2_ref, r), _rt_rows(xs_hbm, pos_ref[p + k]), sem).start(priority=k)
        return carry

    lax.fori_loop(0, DISP_TM, issue, 0, unroll=8)
    for k in range(2):
        pltpu.make_async_copy(h2_ref, _rt_rows(xs_hbm, 0, DISP_TM), sem).wait()


def _dispatch(pos, last_tile, h2):
    return pl.pallas_call(
        _dispatch_kernel,
        grid_spec=pltpu.PrefetchScalarGridSpec(
            num_scalar_prefetch=2,
            grid=(T_ALL // DISP_TM,),
            in_specs=[pl.BlockSpec(_rt_shape(DISP_TM), lambda i, pos, last: (i, 0))],
            out_specs=pl.BlockSpec(memory_space=pl.ANY),
            scratch_shapes=[pltpu.VMEM(_rt_shape(TME), f32), pltpu.SemaphoreType.DMA(())],
        ),
        out_shape=jax.ShapeDtypeStruct(_rt_shape(P_PAD), f32),
        compiler_params=_dma_cparams(),
        name="dispatch",
    )(pos, last_tile, h2)


def _moe_kernel(te_ref, na_ref, x_ref, w1_ref, w3_ref, w2_ref, o_ref, w1_scr, w3_scr, w2_scr):
    i = pl.program_id(0)
    changed = (i == 0) | (te_ref[i] != te_ref[jnp.maximum(i - 1, 0)])

    @pl.when(changed)
    def _():
        w1_scr[...] = w1_ref[...].astype(bf16)
        w3_scr[...] = w3_ref[...].astype(bf16)
        w2_scr[...] = w2_ref[...].astype(bf16)

    @pl.when(i < na_ref[0])
    def _():
        x = _from_row_tiles(x_ref).astype(bf16)
        hid = _silu(_dot(x, w1_scr[...])) * _dot(x, w3_scr[...])
        _to_row_tiles(o_ref, _dot(hid.astype(bf16), w2_scr[...]))

    @pl.when(i >= na_ref[0])
    def _():
        o_ref[...] = jnp.zeros(o_ref.shape, f32)


def _moe(tile_expert, n_active, x_sorted, w1, w3, w2, layer):
    wspec = lambda shape: pl.BlockSpec((None, None) + shape, lambda i, te, na: (layer, te[i], 0, 0))
    xmap = lambda i, te, na: (jnp.maximum(jnp.minimum(i, na[0] - 1), 0), 0)
    return pl.pallas_call(
        _moe_kernel,
        grid_spec=pltpu.PrefetchScalarGridSpec(
            num_scalar_prefetch=2,
            grid=(N_ETILES,),
            in_specs=[
                pl.BlockSpec(_rt_shape(TME), xmap),
                wspec((D_MODEL, MOE_D_FF)), wspec((D_MODEL, MOE_D_FF)), wspec((MOE_D_FF, D_MODEL)),
            ],
            out_specs=pl.BlockSpec(_rt_shape(TME), lambda i, te, na: (i, 0)),
            scratch_shapes=[pltpu.VMEM((D_MODEL, MOE_D_FF), bf16), pltpu.VMEM((D_MODEL, MOE_D_FF), bf16),
                            pltpu.VMEM((MOE_D_FF, D_MODEL), bf16)],
        ),
        out_shape=jax.ShapeDtypeStruct(_rt_shape(P_PAD), f32),
        compiler_params=_cparams(1),
        name="moe",
    )(tile_expert, n_active, x_sorted, w1, w3, w2)


def _combine_kernel(pos_ref, x1_ref, route_ref, mod_ref, lng_ref, lnb_ref, y_hbm, oc_ref, od_ref, buf, sem):
    i = pl.program_id(0)
    slot = i % 2

    def gather_tile(tile, into):
        def issue(r, carry):
            p = (tile * TM + r) * 2
            for k in range(2):
                pltpu.make_async_copy(_rt_rows(y_hbm, pos_ref[p + k]), _rt_rows(buf.at[into, k], r),
                                      sem.at[into]).start(priority=k)
            return carry

        lax.fori_loop(0, TM, issue, 0, unroll=8)

    @pl.when(i == 0)
    def _():
        gather_tile(0, 0)

    @pl.when(i + 1 < pl.num_programs(0))
    def _():
        gather_tile(i + 1, 1 - slot)

    m = mod_ref[pl.ds(_row_type(i, TM), 1), :]
    g2 = m[:, 5 * D_MODEL:6 * D_MODEL]
    route = route_ref[...]
    for k in range(2):
        pltpu.make_async_copy(_rt_rows(y_hbm, 0, TM), buf.at[slot, k], sem.at[slot]).wait()
    ff = _from_row_tiles(buf.at[slot, 0]) * route[:, 2:3] + _from_row_tiles(buf.at[slot, 1]) * route[:, 3:4]
    out = _layer_norm(DEEPNORM_ALPHA * x1_ref[...] + g2 * ff, lng_ref[...], lnb_ref[...])

    @pl.when(i < N_CTX_TILES)
    def _():
        oc_ref[...] = out

    @pl.when(i >= N_CTX_TILES)
    def _():
        od_ref[...] = out


def _combine(pos, x1, route, mod, ln_g, ln_b, y_sorted):
    const = lambda shape: pl.BlockSpec(shape, lambda i, pos: (0,) * len(shape))
    row = lambda width: pl.BlockSpec((TM, width), lambda i, pos: (i, 0))
    return pl.pallas_call(
        _combine_kernel,
        grid_spec=pltpu.PrefetchScalarGridSpec(
            num_scalar_prefetch=1,
            grid=(T_ALL // TM,),
            in_specs=[row(D_MODEL), row(LANES), const(mod.shape), const(ln_g.shape), const(ln_b.shape),
                      pl.BlockSpec(memory_space=pl.ANY)],
            out_specs=_x_specs(),
            scratch_shapes=[pltpu.VMEM((2, 2) + _rt_shape(TM), f32), pltpu.SemaphoreType.DMA((2,))],
        ),
        out_shape=[jax.ShapeDtypeStruct((T_CTX, D_MODEL), f32), jax.ShapeDtypeStruct((T_DN, D_MODEL), f32)],
        compiler_params=_dma_cparams(),
        name="combine",
    )(pos, x1, route, mod, ln_g, ln_b, y_sorted)


def _pad_lanes(row):
    return jnp.pad(row, ((0, 0), (0, LANES - row.shape[1])))


def kernel(x_prompt, x_sample, cache_k, cache_v, state_ssm, c, c_ctx, w_ada, b_ada, w_in, na_rpb, ssm_conv_w, ssm_conv_b, ssm_a_log, ssm_dt_bias, ssm_d, ssm_norm_g, gmlp_ln_g, gmlp_ln_b, gmlp_ws, gmlp_bs, w_branch, w_o, ln_g, ln_b, moe_w_coarse, moe_w_fine, moe_w1, moe_w3, moe_w2):
    cvec = jnp.concatenate([c_ctx[None], c, jnp.zeros((N_ROWTYPES - 1 - DEC_BATCH, D_MODEL), f32)], axis=0)
    mods = _mods(cvec, w_ada, b_ada)
    cos, sin = _rope_tables()
    xc, xd = x_prompt.reshape(T_CTX, D_MODEL), x_sample.reshape(T_DN, D_MODEL)
    w_pad = jnp.pad(w_in, ((0, 0), (0, 0), (0, W_PAD - N_IN)))
    kc = vc = st = None
    for l in range(DEPTH):
        mod = mods[l]
        proj = _inproj(xc, xd, mod, w_pad, l)

        ya, kc, vc = _ctx_attn(proj, l, kc, vc)
        ya = _na_attn(proj, cache_k, cache_v, _na_bias(na_rpb[l]), cos, sin, l, ya)

        ssd_params = (
            ssm_conv_w[l], ssm_conv_b[l][None],
            _pad_lanes(ssm_a_log[l].reshape(1, -1)), _pad_lanes(ssm_dt_bias[l].reshape(1, -1)),
            jnp.repeat(ssm_d[l], SSM_HEAD_DIM)[None], ssm_norm_g[l][None],
        )
        if st is None:
            ys, st = _ssd(proj, ssd_params, l, latent=False)
        else:
            ys, st = _ssd(proj, ssd_params, l, latent=False, st_prev=st)
        (ys,) = _ssd(proj, ssd_params, l, latent=True, h0=state_ssm, y_prev=ys)

        bs_full = jnp.repeat(gmlp_bs[l].T, GMLP_WIDTH // GMLP_GROUPS, axis=1)
        yg = _gmlp(proj, gmlp_ln_g[l][None], gmlp_ln_b[l][None], gmlp_ws[l], bs_full)

        w_r = _pad_lanes(jnp.concatenate([moe_w_coarse[l], moe_w_fine[l]], axis=1))
        w_r_hi = w_r.astype(bf16)
        w_r = jnp.concatenate([w_r_hi, (w_r - w_r_hi.astype(f32)).astype(bf16)], axis=1)
        x1, h2, route = _merge(xc, xd, ya, ys, yg, mod, w_pad, w_branch[l].astype(bf16), w_o[l].astype(bf16),
                               ln_g[l, 0][None], ln_b[l, 0][None], w_r, l)

        pos, tile_expert, n_active, last_tile = _plan(route)
        x_sorted = _dispatch(pos, last_tile, h2)
        y_sorted = _moe(tile_expert, n_active, x_sorted, moe_w1, moe_w3, moe_w2, l)
        xc, xd = _combine(pos, x1, route, mod, ln_g[l, 1][None], ln_b[l, 1][None], y_sorted)

    return (xc.reshape(BATCH, SEQ, D_MODEL), xd.reshape(DEC_BATCH, DEC_SEQ, D_MODEL), kc, vc, st)
```
